```python
import math
import jax, jax.numpy as jnp
from jax import lax
import numpy as np

D_MODEL = 1024
BATCH = 16
SEQ = 2048
DEPTH = 4

GRID_W = 64
CTX_LEN = 256
HEAD_DIM = 64
N_FREQ = HEAD_DIM // 4
ROPE_BASE = 10000.0
Q_BLOCK = 128
BRANCH_WIDTH = D_MODEL // 2
N_BRANCHES = 3
A_Q_HEADS = BRANCH_WIDTH // HEAD_DIM
A_KV_HEADS = 2
A_GROUP = A_Q_HEADS // A_KV_HEADS
B_DV = 128
B_HEADS = BRANCH_WIDTH // B_DV
B_DK = 64
GLA_RANK = 16
GLA_TAU = 16.0
GLA_CHUNK = 64
C_HEADS = BRANCH_WIDTH // (2 * HEAD_DIM)
FFN_DIM = 2816
CONV_WIDTH = 3
ADA_CHUNKS = 6
EPS = 1e-6

PROJ_SIZES = (
    A_Q_HEADS * HEAD_DIM, A_KV_HEADS * HEAD_DIM, A_KV_HEADS * HEAD_DIM,
    B_HEADS * B_DK, B_HEADS * B_DK, B_HEADS * B_DV, B_HEADS * B_DV,
    2 * GLA_RANK,
    C_HEADS * 2 * HEAD_DIM, C_HEADS * 2 * HEAD_DIM, C_HEADS * 2 * HEAD_DIM,
    N_BRANCHES * D_MODEL,
)
PROJ_TOTAL = sum(PROJ_SIZES)
SPLIT_POINTS = tuple(int(v) for v in np.cumsum(PROJ_SIZES)[:-1])

kernel_name = 'hybrid_gated_parallel_mixer_dit_block'


def rms_norm(x, g):
    xf = x.astype(jnp.float32)
    y = xf * lax.rsqrt(jnp.mean(xf * xf, axis=-1, keepdims=True) + EPS)
    return (y * g.astype(jnp.float32)).astype(x.dtype)


def axial_rope_tables(rows):
    row = jnp.repeat(jnp.arange(rows, dtype=jnp.float32), GRID_W)
    col = jnp.tile(jnp.arange(GRID_W, dtype=jnp.float32), rows)
    inv_freq = ROPE_BASE ** (-jnp.arange(N_FREQ, dtype=jnp.float32) / N_FREQ)
    ang = jnp.concatenate([row[:, None] * inv_freq, col[:, None] * inv_freq], axis=-1)
    return jnp.cos(ang), jnp.sin(ang)


def apply_axial_rope(x, cos, sin):
    xa = x.reshape(x.shape[:-1] + (2, 2, N_FREQ))
    bshape = (x.shape[1],) + (1,) * (x.ndim - 3) + (2, 1, N_FREQ)
    c = cos.reshape(bshape).astype(x.dtype)
    s = sin.reshape(bshape).astype(x.dtype)
    x1 = xa[..., 0:1, :]
    x2 = xa[..., 1:2, :]
    rot = jnp.concatenate([-x2, x1], axis=-2)
    return (xa * c + rot * s).reshape(x.shape)


def split_projection(z):
    b, s, _ = z.shape
    qa, ka, va, qb, kb, vb, gb, rb, qc, kc, vc, gt = jnp.split(z, SPLIT_POINTS, axis=-1)
    qa = qa.reshape(b, s, A_KV_HEADS, A_GROUP, HEAD_DIM)
    ka = ka.reshape(b, s, A_KV_HEADS, HEAD_DIM)
    va = va.reshape(b, s, A_KV_HEADS, HEAD_DIM)
    qb = qb.reshape(b, s, B_HEADS, B_DK)
    kb = kb.reshape(b, s, B_HEADS, B_DK)
    vb = vb.reshape(b, s, B_HEADS, B_DV)
    gb = gb.reshape(b, s, B_HEADS, B_DV)
    qc = qc.reshape(b, s, C_HEADS, 2, HEAD_DIM)
    kc = kc.reshape(b, s, C_HEADS, 2, HEAD_DIM)
    vc = vc.reshape(b, s, C_HEADS, 2 * HEAD_DIM)
    return qa, ka, va, qb, kb, vb, gb, rb, qc, kc, vc, gt


def dual_attention(qa, ka, va, qc, kc, vc, lam):
    sa = jnp.einsum('bqhgd,bkhd->bhgqk', qa, ka).astype(jnp.float32)
    pa = jax.nn.softmax(sa, axis=-1).astype(va.dtype)
    oa = jnp.einsum('bhgqk,bkhd->bqhgd', pa, va)
    sc = jnp.einsum('bqhmd,bkhmd->bhmqk', qc, kc).astype(jnp.float32)
    pc = jax.nn.softmax(sc, axis=-1)
    pd = (pc[:, :, 0] - lam * pc[:, :, 1]).astype(vc.dtype)
    oc = jnp.einsum('bhqk,bkhe->bqhe', pd, vc)
    return oa, oc


def to_blocks(t):
    b, s = t.shape[:2]
    return jnp.moveaxis(t.reshape((b, s // Q_BLOCK, Q_BLOCK) + t.shape[2:]), 1, 0)


def from_blocks(t):
    t = jnp.moveaxis(t, 0, 1)
    return t.reshape((t.shape[0], -1) + t.shape[3:])


def gla_log_decay(r, w_dec, b_dec):
    b, s, _ = r.shape
    r_f, r_b = jnp.split(r, 2, axis=-1)
    def one(rr, w, bias):
        z = (rr @ w + bias).astype(jnp.float32)
        return (jax.nn.log_sigmoid(z) / GLA_TAU).reshape(b, s, B_HEADS, B_DK)
    return one(r_f, w_dec[0], b_dec[0]), one(r_b, w_dec[1], b_dec[1])


def gla_chunked(q, k, v, g, s0):
    b, h, s, dk = q.shape
    dv = v.shape[-1]
    n = s // GLA_CHUNK
    q = q.reshape(b, h, n, GLA_CHUNK, dk)
    k = k.reshape(b, h, n, GLA_CHUNK, dk)
    v = v.reshape(b, h, n, GLA_CHUNK, dv)
    G = jnp.cumsum(g.reshape(b, h, n, GLA_CHUNK, dk), axis=3)
    G_last = G[..., -1:, :]
    q_d = q * jnp.exp(G)
    k_d = k * jnp.exp(-G)
    k_end = k * jnp.exp(G_last - G)
    causal = jnp.tril(jnp.ones((GLA_CHUNK, GLA_CHUNK), dtype=bool))
    att = jnp.where(causal, jnp.einsum('bhnid,bhnjd->bhnij', q_d, k_d), 0.0)
    o_intra = jnp.einsum('bhnij,bhnje->bhnie', att, v)
    u = jnp.einsum('bhncd,bhnce->bhnde', k_end, v)
    decay = jnp.exp(G_last[..., 0, :])
    def step(state, inp):
        dec, inc = inp
        return dec[..., None] * state + inc, state
    s_fin, s_start = lax.scan(step, s0, (jnp.moveaxis(decay, 2, 0), jnp.moveaxis(u, 2, 0)))
    s_start = jnp.moveaxis(s_start, 0, 2)
    o_inter = jnp.einsum('bhncd,bhnde->bhnce', q_d, s_start)
    return (o_intra + o_inter).reshape(b, h, s, dv), s_fin


def gla_bidirectional(q, k, v, la_f, la_b, s0_f, s0_b):
    heads_first = lambda a: jnp.swapaxes(a, 1, 2).astype(jnp.float32)
    rev = lambda a: jnp.flip(a, axis=2)
    qh, kh, vh, gf, gb = (heads_first(a) for a in (q, k, v, la_f, la_b))
    o_f, s_f = gla_chunked(qh, kh, vh, gf, s0_f)
    o_b, s_b = gla_chunked(rev(qh), rev(kh), rev(vh), rev(gb), s0_b)
    o = jnp.swapaxes(o_f + rev(o_b), 1, 2).astype(v.dtype)
    return o, s_f, s_b


def merge_branches(ya, yb, yc, gates, w_br, w_o):
    b, s, _ = gates.shape
    g = jax.nn.sigmoid(gates.reshape(b, s, N_BRANCHES, D_MODEL))
    y = jnp.stack([ya, yb, yc], axis=2)
    br = jnp.einsum('bsnc,ncd->bsnd', y, w_br)
    return jnp.einsum('bsd,de->bse', jnp.sum(g * br, axis=2), w_o)


def branch_outputs(oa, ob, gb, oc, gla_g, diff_g, lambda_init):
    b, s = oa.shape[:2]
    ya = oa.reshape(b, s, BRANCH_WIDTH)
    yb = (rms_norm(ob, gla_g) * jax.nn.silu(gb)).reshape(b, s, BRANCH_WIDTH)
    yc = (rms_norm(oc, diff_g) * (1.0 - lambda_init)).reshape(b, s, BRANCH_WIDTH)
    return ya, yb, yc


def token_mixer(u_ctx, u_lat, w_in, qk_g, w_dec, b_dec, gla_g, lam_p, diff_g, w_br, w_o,
                cos, sin, lambda_init, need_ctx):
    qa_c, ka_c, va_c, qb_c, kb_c, vb_c, gb_c, rb_c, qc_c, kc_c, vc_c, gt_c = split_projection(u_ctx @ w_in)
    qa_l, ka_l, va_l, qb_l, kb_l, vb_l, gb_l, rb_l, qc_l, kc_l, vc_l, gt_l = split_projection(u_lat @ w_in)
    scale = HEAD_DIM ** -0.5
    qa_c = rms_norm(qa_c, qk_g[0]) * scale
    ka_c = rms_norm(ka_c, qk_g[1])
    qa_l = apply_axial_rope(rms_norm(qa_l, qk_g[0]), cos, sin) * scale
    ka_l = apply_axial_rope(rms_norm(ka_l, qk_g[1]), cos, sin)
    qc_c = qc_c * scale
    qc_l = apply_axial_rope(qc_l, cos, sin) * scale
    kc_l = apply_axial_rope(kc_l, cos, sin)
    lp = lam_p.astype(jnp.float32)
    lam = jnp.exp(jnp.sum(lp[0] * lp[1])) - jnp.exp(jnp.sum(lp[2] * lp[3])) + lambda_init
    ka_all = jnp.concatenate([ka_c, ka_l], axis=1)
    va_all = jnp.concatenate([va_c, va_l], axis=1)
    kc_all = jnp.concatenate([kc_c, kc_l], axis=1)
    vc_all = jnp.concatenate([vc_c, vc_l], axis=1)
    oa_l, oc_l = lax.map(
        lambda qs: dual_attention(qs[0], ka_all, va_all, qs[1], kc_all, vc_all, lam),
        (to_blocks(qa_l), to_blocks(qc_l)))
    oa_l, oc_l = from_blocks(oa_l), from_blocks(oc_l)
    sb = B_DK ** -0.5
    laf_c, lab_c = gla_log_decay(rb_c, w_dec, b_dec)
    laf_l, lab_l = gla_log_decay(rb_l, w_dec, b_dec)
    zero_state = jnp.zeros((u_ctx.shape[0], B_HEADS, B_DK, B_DV), jnp.float32)
    ob_c, s_f, s_b = gla_bidirectional(qb_c * sb, kb_c, vb_c, laf_c, lab_c, zero_state, zero_state)
    ob_l, _, _ = gla_bidirectional(qb_l * sb, kb_l, vb_l, laf_l, lab_l, s_f, s_b)
    ya, yb, yc = branch_outputs(oa_l, ob_l, gb_l, oc_l, gla_g, diff_g, lambda_init)
    y_lat = merge_branches(ya, yb, yc, gt_l, w_br, w_o)
    if not need_ctx:
        return None, y_lat
    oa_c, oc_c = dual_attention(qa_c, ka_c, va_c, qc_c, kc_c, vc_c, lam)
    ya, yb, yc = branch_outputs(oa_c, ob_c, gb_c, oc_c, gla_g, diff_g, lambda_init)
    y_ctx = merge_branches(ya, yb, yc, gt_c, w_br, w_o)
    return y_ctx, y_lat


def depthwise_conv(a, w, b):
    out = lax.conv_general_dilated(
        a, w[:, None, :], window_strides=(1,),
        padding=((CONV_WIDTH // 2, CONV_WIDTH // 2),),
        dimension_numbers=('NWC', 'WIO', 'NWC'),
        feature_group_count=a.shape[-1])
    return out + b


def conv_ffn(u, w_up, conv_w, conv_b, w_down):
    gate, val = jnp.split(u @ w_up, 2, axis=-1)
    gate = depthwise_conv(gate, conv_w, conv_b)
    return (jax.nn.silu(gate) * val) @ w_down


def setup_inputs(seed: int = 0) -> dict:
    key = jax.random.key(seed)
    ks = jax.random.split(key, 20)
    nrm = jax.random.normal
    f = jnp.float32
    return {
        'x': nrm(ks[0], (BATCH, SEQ, D_MODEL), f),
        'c': nrm(ks[1], (BATCH, D_MODEL), f),
        'ctx': nrm(ks[2], (BATCH, CTX_LEN, D_MODEL), f),
        'c_ctx': nrm(ks[3], (D_MODEL,), f),
        'ada_w': nrm(ks[4], (DEPTH, D_MODEL, ADA_CHUNKS * D_MODEL), f) * (0.5 * D_MODEL ** -0.5),
        'ada_b': 0.02 * nrm(ks[5], (DEPTH, ADA_CHUNKS * D_MODEL), f),
        'norm_g': 1.0 + 0.05 * nrm(ks[6], (DEPTH, 4, D_MODEL), f),
        'w_in': nrm(ks[7], (DEPTH, D_MODEL, PROJ_TOTAL), f) * D_MODEL ** -0.5,
        'qk_norm_a': 1.0 + 0.05 * nrm(ks[8], (DEPTH, 2, HEAD_DIM), f),
        'gla_w_decay': nrm(ks[9], (DEPTH, 2, GLA_RANK, B_HEADS * B_DK), f) * GLA_RANK ** -0.5,
        'gla_b_decay': 0.1 * nrm(ks[10], (DEPTH, 2, B_HEADS * B_DK), f),
        'gla_norm': 1.0 + 0.05 * nrm(ks[11], (DEPTH, B_DV), f),
        'diff_lambda': 0.1 * nrm(ks[12], (DEPTH, 4, HEAD_DIM), f),
        'diff_norm': 1.0 + 0.05 * nrm(ks[13], (DEPTH, 2 * HEAD_DIM), f),
        'w_branch': nrm(ks[14], (DEPTH, N_BRANCHES, BRANCH_WIDTH, D_MODEL), f) * BRANCH_WIDTH ** -0.5,
        'w_out': nrm(ks[15], (DEPTH, D_MODEL, D_MODEL), f) * D_MODEL ** -0.5,
        'w_ffn_in': nrm(ks[16], (DEPTH, D_MODEL, 2 * FFN_DIM), f) * D_MODEL ** -0.5,
        'ffn_conv_w': nrm(ks[17], (DEPTH, CONV_WIDTH, FFN_DIM), f) * CONV_WIDTH ** -0.5,
        'ffn_conv_b': 0.02 * nrm(ks[18], (DEPTH, FFN_DIM), f),
        'w_ffn_out': nrm(ks[19], (DEPTH, FFN_DIM, D_MODEL), f) * FFN_DIM ** -0.5,
    }


def reference(x, c, ctx, c_ctx, ada_w, ada_b, norm_g, w_in, qk_norm_a, gla_w_decay, gla_b_decay,
              gla_norm, diff_lambda, diff_norm, w_branch, w_out, w_ffn_in, ffn_conv_w, ffn_conv_b,
              w_ffn_out):
    rows = x.shape[1] // GRID_W
    cos, sin = axial_rope_tables(rows)
    silu_c = jax.nn.silu(c)
    silu_cc = jax.nn.silu(c_ctx)
    h_lat, h_ctx = x, ctx
    for layer in range(DEPTH):
        need_ctx = layer < DEPTH - 1
        lambda_init = 0.8 - 0.6 * math.exp(-0.3 * layer)
        sh1, sc1, g1, sh2, sc2, g2 = jnp.split((silu_c @ ada_w[layer] + ada_b[layer])[:, None, :], ADA_CHUNKS, axis=-1)
        csh1, csc1, cg1, csh2, csc2, cg2 = jnp.split(silu_cc @ ada_w[layer] + ada_b[layer], ADA_CHUNKS, axis=-1)
        g_pre1, g_post1, g_pre2, g_post2 = norm_g[layer]
        u_lat = rms_norm(h_lat, g_pre1) * (1.0 + sc1) + sh1
        u_ctx = rms_norm(h_ctx, g_pre1) * (1.0 + csc1) + csh1
        y_ctx, y_lat = token_mixer(
            u_ctx, u_lat, w_in[layer], qk_norm_a[layer], gla_w_decay[layer], gla_b_decay[layer],
            gla_norm[layer], diff_lambda[layer], diff_norm[layer], w_branch[layer], w_out[layer],
            cos, sin, lambda_init, need_ctx)
        h_lat = h_lat + g1 * rms_norm(y_lat, g_post1)
        v_lat = rms_norm(h_lat, g_pre2) * (1.0 + sc2) + sh2
        h_lat = h_lat + g2 * rms_norm(conv_ffn(v_lat, w_ffn_in[layer], ffn_conv_w[layer], ffn_conv_b[layer], w_ffn_out[layer]), g_post2)
        if need_ctx:
            h_ctx = h_ctx + cg1 * rms_norm(y_ctx, g_post1)
            v_ctx = rms_norm(h_ctx, g_pre2) * (1.0 + csc2) + csh2
            h_ctx = h_ctx + cg2 * rms_norm(conv_ffn(v_ctx, w_ffn_in[layer], ffn_conv_w[layer], ffn_conv_b[layer], w_ffn_out[layer]), g_post2)
    return h_lat
```

```python
import functools
import math

import numpy as np
import jax
import jax.numpy as jnp
from jax import lax
from jax.experimental import pallas as pl
from jax.experimental.pallas import tpu as pltpu

F32 = jnp.float32
BF16 = jnp.bfloat16

GRID_W = 64
HEAD_DIM = 64
N_FREQ = HEAD_DIM // 4
ROPE_BASE = 10000.0
A_Q_HEADS = 8
A_KV_HEADS = 2
B_HEADS = 4
B_DK = 64
B_DV = 128
GLA_RANK = 16
GLA_TAU = 16.0
GLA_CHUNK = 64
C_HEADS = 4
N_BRANCHES = 3
ADA_CHUNKS = 6
EPS = 1e-6

LANES = 128
SUBLANES = 8
VMEM_LIMIT_BYTES = 56 * 1024 * 1024

TOK = 256
ROW_CHUNK = 16

W_QA, W_KA, W_VA = 512, 128, 128
W_QB, W_KB, W_VB, W_GB = 256, 256, 512, 512
W_QC, W_KC, W_VC = 512, 512, 512
W_RB = LANES
OFF_QA = 0
OFF_KA = OFF_QA + W_QA
OFF_VA = OFF_KA + W_KA
OFF_QB = OFF_VA + W_VA
OFF_KB = OFF_QB + W_QB
OFF_VB = OFF_KB + W_KB
OFF_GB = OFF_VB + W_VB
OFF_QC = OFF_GB + W_GB
OFF_KC = OFF_QC + W_QC
OFF_VC = OFF_KC + W_KC
OFF_RB = OFF_VC + W_VC
W_PROJ = OFF_RB + W_RB


def _cparams(n_axes):
    return pltpu.CompilerParams(
        dimension_semantics=("arbitrary",) * n_axes,
        vmem_limit_bytes=VMEM_LIMIT_BYTES)


def _dot(a, b):
    return jnp.dot(a, b, preferred_element_type=F32)


def _dot_nt(a, b):
    return lax.dot_general(a, b, (((1,), (1,)), ((), ())), preferred_element_type=F32)


def _split2(x):
    hi = x.astype(BF16)
    lo = (x - hi.astype(F32)).astype(BF16)
    return hi, lo


def _rms(x):
    return x * lax.rsqrt(jnp.mean(x * x, axis=-1, keepdims=True) + EPS)


def _silu(x):
    return x * jax.nn.sigmoid(x)


def _lane_lo_mask(shape):
    lane = lax.broadcasted_iota(jnp.int32, shape, len(shape) - 1)
    return (lane % LANES) < HEAD_DIM


def _ada_kernel(c_ref, w_ref, b_ref, o_ref):
    a = _silu(c_ref[...]).astype(BF16)
    o_ref[...] = _dot(a, w_ref[...].astype(BF16)) + b_ref[...]


def _ada_call(cvec, ada_w, ada_b):
    depth, d, n = ada_w.shape
    rows = cvec.shape[0]
    tn = 1536
    return pl.pallas_call(
        _ada_kernel,
        grid=(depth, n // tn),
        in_specs=[
            pl.BlockSpec((rows, d), lambda l, j: (0, 0)),
            pl.BlockSpec((None, d, tn), lambda l, j: (l, 0, j)),
            pl.BlockSpec((None, 1, tn), lambda l, j: (l, 0, j)),
        ],
        out_specs=pl.BlockSpec((None, rows, tn), lambda l, j: (l, 0, j)),
        out_shape=jax.ShapeDtypeStruct((depth, rows, n), F32),
        compiler_params=_cparams(2),
        name="ada",
    )(cvec, ada_w, ada_b.reshape(depth, 1, n))


def _modulated(h, mods, g, shift_row, scale_row):
    return (_rms(h) * g) * (1.0 + mods[scale_row:scale_row + 1, :]) + mods[shift_row:shift_row + 1, :]


def _rope(x, c, sa, sb):
    outs = []
    for j in range(x.shape[1] // LANES):
        xg = x[:, j * LANES:(j + 1) * LANES]
        outs.append(xg * c + pltpu.roll(xg, 16, 1) * sa + pltpu.roll(xg, LANES - 16, 1) * sb)
    return outs[0] if len(outs) == 1 else jnp.concatenate(outs, axis=1)


def _head_rms(q, bd, g):
    hi, lo = _split2(q * q)
    ss = _dot(hi, bd) + _dot(lo, bd)
    return q * lax.rsqrt(ss * (1.0 / HEAD_DIM) + EPS) * g


def _proj_kernel(h_ref, mods_ref, g_ref, w_ref, bd_ref, qkg_ref, wdec_ref, bdec_ref,
                 c_ref, sa_ref, sb_ref,
                 qa_ref, ka_ref, va_ref, qb_ref, kb_ref, vb_ref, gb_ref, qc_ref, kc_ref, vc_ref,
                 la_ref):
    u = _modulated(h_ref[...], mods_ref[...], g_ref[...], 0, 1).astype(BF16)
    c, sa, sb = c_ref[...], sa_ref[...], sb_ref[...]

    def seg(off, width):
        return _dot(u, w_ref[:, off:off + width])

    scale = HEAD_DIM ** -0.5
    qa = _head_rms(seg(OFF_QA, W_QA), bd_ref[...], qkg_ref[0:1, :])
    qa_ref[...] = (_rope(qa, c, sa, sb) * scale).astype(BF16)
    ka = _head_rms(seg(OFF_KA, W_KA), bd_ref[0:W_KA, 0:W_KA], qkg_ref[1:2, 0:W_KA])
    ka_ref[...] = _rope(ka, c, sa, sb).astype(BF16)
    va_ref[...] = seg(OFF_VA, W_VA).astype(BF16)
    qb_ref[...] = seg(OFF_QB, W_QB)
    kb_ref[...] = seg(OFF_KB, W_KB)
    vb_ref[...] = seg(OFF_VB, W_VB).astype(BF16)
    gb_ref[...] = seg(OFF_GB, W_GB)
    qc_ref[...] = (_rope(seg(OFF_QC, W_QC), c, sa, sb) * scale).astype(BF16)
    kc_ref[...] = _rope(seg(OFF_KC, W_KC), c, sa, sb).astype(BF16)
    vc_ref[...] = seg(OFF_VC, W_VC).astype(BF16)
    z = _dot(seg(OFF_RB, W_RB).astype(BF16), wdec_ref[...]) + bdec_ref[...]
    la_ref[...] = (jnp.minimum(z, 0.0) - jnp.log1p(jnp.exp(-jnp.abs(z)))) * (1.0 / GLA_TAU)


def _proj_call(h, mods, g_pre, w_proj, bd, qkg, wdec, bdec, rope_c, rope_sa, rope_sb):
    b, s, d = h.shape
    nblk = s // TOK
    tok = lambda w: pl.BlockSpec((None, TOK, w), lambda bi, i: (bi, i, 0))
    full = lambda a: pl.BlockSpec(a.shape, lambda bi, i: (0,) * a.ndim)
    rope = pl.BlockSpec((TOK, LANES), lambda bi, i: (i, 0))
    widths = [(W_QA, BF16), (W_KA, BF16), (W_VA, BF16), (W_QB, F32), (W_KB, F32), (W_VB, BF16),
              (W_GB, F32), (W_QC, BF16), (W_KC, BF16), (W_VC, BF16), (2 * W_KB, F32)]
    return pl.pallas_call(
        _proj_kernel,
        grid=(b, nblk),
        in_specs=[
            tok(d),
            pl.BlockSpec((None, None, ADA_CHUNKS, d), lambda bi, i: (bi, jnp.minimum(i, 1), 0, 0)),
            full(g_pre), full(w_proj), full(bd), full(qkg), full(wdec), full(bdec),
            rope, rope, rope,
        ],
        out_specs=[tok(w) for w, _ in widths],
        out_shape=[jax.ShapeDtypeStruct((b, s, w), dt) for w, dt in widths],
        compiler_params=_cparams(2),
        name="proj",
    )(h, mods, g_pre, w_proj, bd, qkg, wdec, bdec, rope_c, rope_sa, rope_sb)


def _softmax_rows(s_ref, e_ref, l_ref, nk):
    def body(r, carry):
        rows = pl.ds(pl.multiple_of(r * ROW_CHUNK, ROW_CHUNK), ROW_CHUNK)
        s = s_ref[rows, 0:nk]
        e = jnp.exp(s - jnp.max(s, axis=-1, keepdims=True))
        l_ref[rows, :] = jnp.sum(e, axis=-1, keepdims=True)
        e_ref[rows, 0:nk] = e.astype(e_ref.dtype)
        return carry
    lax.fori_loop(0, TOK // ROW_CHUNK, body, 0)


def _diff_rows(e1_ref, e2_ref, l1_ref, l2_ref, lam, p_ref, nk):
    def body(r, carry):
        rows = pl.ds(pl.multiple_of(r * ROW_CHUNK, ROW_CHUNK), ROW_CHUNK)
        a = e1_ref[rows, 0:nk] * (1.0 / l1_ref[rows, :])
        b = e2_ref[rows, 0:nk] * (lam / l2_ref[rows, :])
        p_ref[rows, 0:nk] = (a - b).astype(BF16)
        return carry
    lax.fori_loop(0, TOK // ROW_CHUNK, body, 0)


def _attn_body(nk, lambda_init, qa_ref, qc_ref, ka_ref, va_ref, kc_ref, vc_ref, lam_ref, dg_ref,
               ya_ref, yc_ref, s1_ref, s2_ref, p_ref, l1_ref, l2_ref):
    lo = _lane_lo_mask((TOK, LANES))
    lo_b = lo.astype(BF16)
    hi_b = 1.0 - lo_b

    def scores(q_grp, k, first, out_ref):
        out_ref[:, 0:nk] = _dot_nt(q_grp * (lo_b if first else hi_b), k)

    ka = ka_ref[0:nk, :]
    va = va_ref[0:nk, :]
    for j in range(A_Q_HEADS // 2):
        q_grp = qa_ref[:, j * LANES:(j + 1) * LANES]
        halves = []
        for first in (True, False):
            scores(q_grp, ka, first, s1_ref)
            _softmax_rows(s1_ref, p_ref, l1_ref, nk)
            halves.append(_dot(p_ref[:, 0:nk], va) * (1.0 / l1_ref[...]))
        ya_ref[:, j * LANES:(j + 1) * LANES] = jnp.where(lo, halves[0], halves[1]).astype(BF16)

    lp = lam_ref[...]
    lam = (jnp.exp(jnp.sum(lp[0:1] * lp[1:2], axis=-1, keepdims=True))
           - jnp.exp(jnp.sum(lp[2:3] * lp[3:4], axis=-1, keepdims=True)) + lambda_init)
    for h in range(C_HEADS):
        grp = slice(h * LANES, (h + 1) * LANES)
        q_grp = qc_ref[:, grp]
        k = kc_ref[0:nk, grp]
        scores(q_grp, k, True, s1_ref)
        _softmax_rows(s1_ref, s1_ref, l1_ref, nk)
        scores(q_grp, k, False, s2_ref)
        _softmax_rows(s2_ref, s2_ref, l2_ref, nk)
        _diff_rows(s1_ref, s2_ref, l1_ref, l2_ref, lam, p_ref, nk)
        o = _dot(p_ref[:, 0:nk], vc_ref[0:nk, grp])
        yc_ref[:, grp] = (_rms(o) * dg_ref[...] * (1.0 - lambda_init)).astype(BF16)


def _attn_kernel(n_ctx, first_blk, lambda_init, *refs):
    n_all = refs[2].shape[0]
    if first_blk == 0:
        blk = pl.program_id(1)

        @pl.when(blk == 0)
        def _():
            _attn_body(n_ctx, lambda_init, *refs)

        @pl.when(blk > 0)
        def _():
            _attn_body(n_all, lambda_init, *refs)
    else:
        _attn_body(n_all, lambda_init, *refs)


def _attn_call(qa, qc, ka, va, kc, vc, lam_p, diff_g, *, n_ctx, first_blk, lambda_init):
    b, s, _ = qa.shape
    nblk = s // TOK - first_blk
    tok = pl.BlockSpec((None, TOK, 512), lambda bi, i: (bi, i + first_blk, 0))
    out_tok = pl.BlockSpec((None, TOK, 512), lambda bi, i: (bi, i, 0))
    kv = lambda w: pl.BlockSpec((None, s, w), lambda bi, i: (bi, 0, 0))
    full = lambda a: pl.BlockSpec(a.shape, lambda bi, i: (0,) * a.ndim)
    return pl.pallas_call(
        functools.partial(_attn_kernel, n_ctx, first_blk, lambda_init),
        grid=(b, nblk),
        in_specs=[tok, tok, kv(W_KA), kv(W_VA), kv(W_KC), kv(W_VC), full(lam_p), full(diff_g)],
        out_specs=[out_tok, out_tok],
        out_shape=[jax.ShapeDtypeStruct((b, nblk * TOK, 512), BF16)] * 2,
        scratch_shapes=[
            pltpu.VMEM((TOK, s), F32), pltpu.VMEM((TOK, s), F32), pltpu.VMEM((TOK, s), BF16),
            pltpu.VMEM((TOK, 1), F32), pltpu.VMEM((TOK, 1), F32),
        ],
        compiler_params=_cparams(2),
        name="attn",
    )(qa, qc, ka, va, kc, vc, lam_p, diff_g)


def _gla_kernel(reverse, final, *refs):
    if final:
        (q_ref, k_ref, v_ref, g_ref, ob_ref, gate_ref, ng_ref, o_ref, state_ref) = refs
    else:
        (q_ref, k_ref, v_ref, g_ref, o_ref, state_ref) = refs
    n_chunks = TOK // GLA_CHUNK
    dkw = B_HEADS * B_DK
    dvw = B_HEADS * B_DV

    @pl.when(pl.program_id(1) == 0)
    def _():
        state_ref[...] = jnp.zeros_like(state_ref)

    row = lax.broadcasted_iota(jnp.int32, (TOK, TOK), 0)
    col = lax.broadcasted_iota(jnp.int32, (TOK, TOK), 1)
    same = (row // GLA_CHUNK) == (col // GLA_CHUNK)
    tri = same & ((col >= row) if reverse else (col <= row))
    tri_b = tri.astype(BF16)
    tri_t = (same & ((row >= col) if reverse else (row <= col))).astype(BF16)
    same_b = same.astype(BF16)

    q = q_ref[...] * (B_DK ** -0.5)
    k = k_ref[...]
    v = v_ref[...]
    g = g_ref[...]
    g_hi, g_lo = _split2(g)
    cum = _dot(tri_b, g_hi) + _dot(tri_b, g_lo)
    q_df = q * jnp.exp(cum)
    q_d = q_df.astype(BF16)
    k_d = (k * jnp.exp(-cum)).astype(BF16)
    gt_hi, gt_lo = _split2(g.T)
    cum_t = _dot(gt_hi, tri_t) + _dot(gt_lo, tri_t)
    last_t = _dot(gt_hi, same_b) + _dot(gt_lo, same_b)
    k_end_t = k.T * jnp.exp(last_t - cum_t)
    sel = ((lax.broadcasted_iota(jnp.int32, (TOK, n_chunks * LANES), 0) // GLA_CHUNK)
           == (lax.broadcasted_iota(jnp.int32, (TOK, n_chunks * LANES), 1) // LANES)).astype(BF16)
    decay = jnp.exp(_dot(gt_hi, sel) + _dot(gt_lo, sel))

    lo = _lane_lo_mask((TOK, LANES))
    o_parts = []
    for h in range(B_HEADS):
        grp = slice((h // 2) * LANES, (h // 2 + 1) * LANES)
        qm = jnp.where(lo if h % 2 == 0 else ~lo, q_df[:, grp], 0.0).astype(BF16)
        att = jnp.where(tri, _dot_nt(qm, k_d[:, grp]), 0.0).astype(BF16)
        o_parts.append(_dot(att, v[:, h * B_DV:(h + 1) * B_DV]))
    o_intra = jnp.concatenate(o_parts, axis=1)

    tok_chunk = lax.broadcasted_iota(jnp.int32, (dkw, TOK), 1) // GLA_CHUNK
    outs = [None] * n_chunks
    order = range(n_chunks - 1, -1, -1) if reverse else range(n_chunks)
    for c in order:
        rows = slice(c * GLA_CHUNK, (c + 1) * GLA_CHUNK)
        outs[c] = o_intra[rows, :] + _dot(q_d[rows, :], state_ref[...].astype(BF16))
        inc = _dot(jnp.where(tok_chunk == c, k_end_t, 0.0).astype(BF16), v)
        dec = decay[:, c * LANES:(c + 1) * LANES]
        for h in range(B_HEADS):
            hr = slice(h * B_DK, (h + 1) * B_DK)
            hc = slice(h * B_DV, (h + 1) * B_DV)
            state_ref[hr, hc] = dec[hr, :] * state_ref[hr, hc] + inc[hr, hc]
    o = jnp.concatenate(outs, axis=0)

    if final:
        o = o + ob_ref[...]
        gate = _silu(gate_ref[...])
        parts = []
        for h in range(B_HEADS):
            hc = slice(h * B_DV, (h + 1) * B_DV)
            parts.append(_rms(o[:, hc]) * ng_ref[...] * gate[:, hc])
        o_ref[...] = jnp.concatenate(parts, axis=1).astype(o_ref.dtype)
    else:
        o_ref[...] = o


def _gla_call(qb, kb, vb, la, extra, *, reverse):
    b, s, _ = qb.shape
    nblk = s // TOK
    final = extra is not None
    if reverse:
        blk = lambda i: jnp.where(i == 0, 0, nblk - i)
    else:
        blk = lambda i: i
    tok = lambda w, cb=0: pl.BlockSpec((None, TOK, w), lambda bi, i: (bi, blk(i), cb))
    in_specs = [tok(W_QB), tok(W_KB), tok(W_VB), tok(W_KB, 1 if reverse else 0)]
    args = [qb, kb, vb, la]
    if final:
        ob, gb, ng = extra
        in_specs += [tok(W_VB), tok(W_GB), pl.BlockSpec(ng.shape, lambda bi, i: (0, 0))]
        args += [ob, gb, ng]
    return pl.pallas_call(
        functools.partial(_gla_kernel, reverse, final),
        grid=(b, nblk),
        in_specs=in_specs,
        out_specs=tok(W_VB),
        out_shape=jax.ShapeDtypeStruct((b, s, W_VB), BF16 if final else F32),
        scratch_shapes=[pltpu.VMEM((B_HEADS * B_DK, B_HEADS * B_DV), F32)],
        compiler_params=_cparams(2),
        name="gla_bwd" if reverse else "gla_fwd",
    )(*args)


def _merge_kernel(h_ref, mods_ref, g_pre_ref, g_post_ref, ya_ref, yb_ref, yc_ref,
                  wg_ref, wbr_ref, wo_ref, o_ref):
    h = h_ref[...]
    mods = mods_ref[...]
    d = h.shape[1]
    u = _modulated(h, mods, g_pre_ref[...], 0, 1).astype(BF16)
    m = None
    for n, y_ref in enumerate((ya_ref, yb_ref, yc_ref)):
        gate = jax.nn.sigmoid(_dot(u, wg_ref[:, n * d:(n + 1) * d]))
        t = gate * _dot(y_ref[...], wbr_ref[n])
        m = t if m is None else m + t
    y = _dot(m.astype(BF16), wo_ref[...])
    o_ref[...] = h + mods[2:3, :] * (_rms(y) * g_post_ref[...])


def _merge_call(h, mods, g_pre, g_post, ya, yb, yc, w_gates, w_br, w_o, *, first_blk):
    b, s, d = h.shape
    nblk = s // TOK - first_blk
    tok = lambda w: pl.BlockSpec((None, TOK, w), lambda bi, i: (bi, i + first_blk, 0))
    tok0 = lambda w: pl.BlockSpec((None, TOK, w), lambda bi, i: (bi, i, 0))
    full = lambda a: pl.BlockSpec(a.shape, lambda bi, i: (0,) * a.ndim)
    return pl.pallas_call(
        _merge_kernel,
        grid=(b, nblk),
        in_specs=[
            tok(d),
            pl.BlockSpec((None, None, ADA_CHUNKS, d),
                         lambda bi, i: (bi, jnp.minimum(i + first_blk, 1), 0, 0)),
            full(g_pre), full(g_post), tok0(512), tok(512), tok0(512),
            full(w_gates), full(w_br), full(w_o),
        ],
        out_specs=tok0(d),
        out_shape=jax.ShapeDtypeStruct((b, nblk * TOK, d), F32),
        compiler_params=_cparams(2),
        name="merge",
    )(h, mods, g_pre, g_post, ya, yb, yc, w_gates, w_br, w_o)


FFN_COLS = 256


def _ffn_kernel(first_blk, last_blk, h_ref, hp_ref, hn_ref, mods_ref, g_pre_ref, g_post_ref,
                wup_ref, cw_ref, cb_ref, wdn_ref, o_ref, acc_ref):
    blk = pl.program_id(1) + first_blk
    mods = mods_ref[...]
    g_pre = g_pre_ref[...]
    h = h_ref[...]
    f = wdn_ref.shape[0]
    prev_ok = blk >= 2
    next_ok = (blk >= 1) & (blk < last_blk)
    v = _modulated(h, mods, g_pre, 3, 4)
    v_prev = jnp.where(prev_ok, _modulated(hp_ref[...], mods, g_pre, 3, 4), 0.0)
    v_next = jnp.where(next_ok, _modulated(hn_ref[...], mods, g_pre, 3, 4), 0.0)
    vb = v.astype(BF16)
    v_ext = jnp.concatenate([v_prev, v, v_next], axis=0).astype(BF16)
    p0 = SUBLANES
    for c in range(f // FFN_COLS):
        cols = slice(c * FFN_COLS, (c + 1) * FFN_COLS)
        gate = _dot(v_ext, wup_ref[:, cols])
        val = _dot(vb, wup_ref[:, f + c * FFN_COLS:f + (c + 1) * FFN_COLS])
        conv = (gate[p0 - 1:p0 - 1 + TOK, :] * cw_ref[0:1, cols]
                + gate[p0:p0 + TOK, :] * cw_ref[1:2, cols]
                + gate[p0 + 1:p0 + 1 + TOK, :] * cw_ref[2:3, cols]) + cb_ref[:, cols]
        act = (_silu(conv) * val).astype(BF16)
        part = _dot(act, wdn_ref[cols, :])
        if c == 0:
            acc_ref[...] = part
        else:
            acc_ref[...] += part
    o_ref[...] = h + mods[5:6, :] * (_rms(acc_ref[...]) * g_post_ref[...])


def _ffn_call(h, mods, g_pre, g_post, w_up, conv_w, conv_b, w_dn, *, first_blk):
    b, s, d = h.shape
    nblk = s // TOK
    sub = TOK // SUBLANES
    n_sub = s // SUBLANES
    tok = pl.BlockSpec((None, TOK, d), lambda bi, i: (bi, i, 0))
    full = lambda a: pl.BlockSpec(a.shape, lambda bi, i: (0,) * a.ndim)
    prev = pl.BlockSpec((None, SUBLANES, d), lambda bi, i: (bi, jnp.maximum(i * sub - 1, 0), 0))
    nxt = pl.BlockSpec((None, SUBLANES, d), lambda bi, i: (bi, jnp.minimum((i + 1) * sub, n_sub - 1), 0))
    return pl.pallas_call(
        functools.partial(_ffn_kernel, first_blk, nblk - 1 + first_blk),
        grid=(b, nblk),
        in_specs=[
            tok, prev, nxt,
            pl.BlockSpec((None, None, ADA_CHUNKS, d),
                         lambda bi, i: (bi, jnp.minimum(i + first_blk, 1), 0, 0)),
            full(g_pre), full(g_post), full(w_up), full(conv_w), full(conv_b), full(w_dn),
        ],
        out_specs=tok,
        out_shape=jax.ShapeDtypeStruct((b, s, d), F32),
        scratch_shapes=[pltpu.VMEM((TOK, d), F32)],
        compiler_params=_cparams(2),
        name="ffn",
    )(h, h, h, mods, g_pre, g_post, w_up, conv_w, conv_b, w_dn)


def _rope_tables(n_ctx, n_lat):
    rows = n_lat // GRID_W
    row = jnp.repeat(jnp.arange(rows, dtype=F32), GRID_W)
    col = jnp.tile(jnp.arange(GRID_W, dtype=F32), rows)
    inv_freq = ROPE_BASE ** (-jnp.arange(N_FREQ, dtype=F32) / N_FREQ)
    ang = jnp.concatenate([row[:, None] * inv_freq, col[:, None] * inv_freq], axis=-1)
    cos, sin = jnp.cos(ang), jnp.sin(ang)
    zeros = jnp.zeros_like(sin[:, :N_FREQ])

    def head(row_part, col_part_a, col_part_b):
        return jnp.concatenate(row_part + col_part_a + col_part_b, axis=-1)

    cr, cc = cos[:, :N_FREQ], cos[:, N_FREQ:]
    sr, sc = sin[:, :N_FREQ], sin[:, N_FREQ:]
    c = jnp.concatenate([cr, cr, cc, cc], axis=-1)
    sa = jnp.concatenate([zeros, sr, zeros, sc], axis=-1)
    sb = jnp.concatenate([-sr, zeros, -sc, zeros], axis=-1)
    pad = lambda t, fill: jnp.concatenate(
        [jnp.full((n_ctx, HEAD_DIM), fill, F32), t], axis=0)
    two = lambda t: jnp.concatenate([t, t], axis=-1)
    return two(pad(c, 1.0)), two(pad(sa, 0.0)), two(pad(sb, 0.0))


def _pair_heads(n_heads):
    half = n_heads // 2
    order = []
    for j in range(half):
        order += [j, half + j]
    return np.concatenate([np.arange(h * HEAD_DIM, (h + 1) * HEAD_DIM) for h in order])


def kernel(x, c, ctx, c_ctx, ada_w, ada_b, norm_g, w_in, qk_norm_a, gla_w_decay, gla_b_decay,
           gla_norm, diff_lambda, diff_norm, w_branch, w_out, w_ffn_in, ffn_conv_w, ffn_conv_b,
           w_ffn_out):
    bsz, n_lat, d = x.shape
    n_ctx = ctx.shape[1]
    depth = ada_w.shape[0]
    assert n_ctx == TOK and n_lat % TOK == 0 and n_lat % GRID_W == 0
    ffn_dim = w_ffn_out.shape[1]
    assert ffn_dim % FFN_COLS == 0

    perm = _pair_heads(A_Q_HEADS)
    sp = np.cumsum([512, 128, 128, 256, 256, 512, 512, 2 * GLA_RANK, 512, 512, 512])
    qa_w, ka_w, va_w, qb_w, kb_w, vb_w, gb_w, rb_w, qc_w, kc_w, vc_w, gt_w = jnp.split(w_in, sp, axis=-1)
    rb_w = jnp.pad(rb_w, ((0, 0), (0, 0), (0, W_RB - 2 * GLA_RANK)))
    w_proj = jnp.concatenate(
        [qa_w[..., perm], ka_w, va_w, qb_w, kb_w, vb_w, gb_w, qc_w, kc_w, vc_w, rb_w], axis=-1).astype(BF16)
    w_gates = gt_w.astype(BF16)
    w_br = jnp.concatenate([w_branch[:, 0:1, perm, :], w_branch[:, 1:]], axis=1).astype(BF16)
    w_o = w_out.astype(BF16)
    w_up = w_ffn_in.astype(BF16)
    w_dn = w_ffn_out.astype(BF16)
    dk_all = B_HEADS * B_DK
    wdec = jnp.zeros((depth, W_RB, 2 * dk_all), F32)
    wdec = wdec.at[:, 0:GLA_RANK, 0:dk_all].set(gla_w_decay[:, 0])
    wdec = wdec.at[:, GLA_RANK:2 * GLA_RANK, dk_all:].set(gla_w_decay[:, 1]).astype(BF16)
    bdec = gla_b_decay.reshape(depth, 1, 2 * dk_all)
    qkg = jnp.tile(qk_norm_a, (1, 1, A_Q_HEADS))
    head_id = np.arange(W_QA) // HEAD_DIM
    bd = jnp.asarray(head_id[:, None] == head_id[None, :], BF16)
    rope_c, rope_sa, rope_sb = _rope_tables(n_ctx, n_lat)

    rows = -(-(bsz + 1) // SUBLANES) * SUBLANES
    cvec = jnp.zeros((rows, d), F32).at[:bsz].set(c).at[bsz].set(c_ctx)
    ada = _ada_call(cvec, ada_w, ada_b)
    lat = ada[:, :bsz].reshape(depth, bsz, 1, ADA_CHUNKS, d)
    ctx_m = jnp.broadcast_to(ada[:, bsz].reshape(depth, 1, 1, ADA_CHUNKS, d), lat.shape)
    mods_all = jnp.concatenate([ctx_m, lat], axis=2)

    h = jnp.concatenate([ctx, x], axis=1)
    for layer in range(depth):
        last = layer == depth - 1
        first_blk = 1 if last else 0
        lambda_init = 0.8 - 0.6 * math.exp(-0.3 * layer)
        mods = mods_all[layer]
        g = norm_g[layer][:, None, :]
        qa, ka, va, qb, kb, vb, gb, qc, kc, vc, la = _proj_call(
            h, mods, g[0], w_proj[layer], bd, qkg[layer], wdec[layer], bdec[layer],
            rope_c, rope_sa, rope_sb)
        ya, yc = _attn_call(qa, qc, ka, va, kc, vc, diff_lambda[layer], diff_norm[layer][None, :],
                            n_ctx=n_ctx, first_blk=first_blk, lambda_init=lambda_init)
        ob = _gla_call(qb, kb, vb, la, None, reverse=True)
        yb = _gla_call(qb, kb, vb, la, (ob, gb, gla_norm[layer][None, :]), reverse=False)
        h_mid = _merge_call(h, mods, g[0], g[1], ya, yb, yc, w_gates[layer], w_br[layer], w_o[layer],
                            first_blk=first_blk)
        h = _ffn_call(h_mid, mods, g[2], g[3], w_up[layer], ffn_conv_w[layer],
                      ffn_conv_b[layer][None, :], w_dn[layer], first_blk=first_blk)
    return h
```

```python
import functools
import math

import numpy as np
import jax
import jax.numpy as jnp
from jax import lax
from jax.experimental import pallas as pl
from jax.experimental.pallas import tpu as pltpu

F32 = jnp.float32
BF16 = jnp.bfloat16

GRID_W = 64
HEAD_DIM = 64
N_FREQ = HEAD_DIM // 4
ROPE_BASE = 10000.0
A_Q_HEADS = 8
A_KV_HEADS = 2
B_HEADS = 4
B_DK = 64
B_DV = 128
GLA_RANK = 16
GLA_TAU = 16.0
GLA_CHUNK = 64
C_HEADS = 4
N_BRANCHES = 3
ADA_CHUNKS = 6
EPS = 1e-6

LANES = 128
SUBLANES = 8
VMEM_LIMIT_BYTES = 56 * 1024 * 1024

TOK = 256
ROW_CHUNK = 16

W_QA, W_KA, W_VA = 512, 128, 128
W_QB, W_KB, W_VB, W_GB = 256, 256, 512, 512
W_QC, W_KC, W_VC = 512, 512, 512
W_RB = LANES
OFF_QA = 0
OFF_KA = OFF_QA + W_QA
OFF_VA = OFF_KA + W_KA
OFF_QB = OFF_VA + W_VA
OFF_KB = OFF_QB + W_QB
OFF_VB = OFF_KB + W_KB
OFF_GB = OFF_VB + W_VB
OFF_QC = OFF_GB + W_GB
OFF_KC = OFF_QC + W_QC
OFF_VC = OFF_KC + W_KC
OFF_RB = OFF_VC + W_VC
W_PROJ = OFF_RB + W_RB


def _cparams(n_axes):
    return pltpu.CompilerParams(
        dimension_semantics=("arbitrary",) * n_axes,
        vmem_limit_bytes=VMEM_LIMIT_BYTES)


def _dot(a, b):
    return jnp.dot(a, b, preferred_element_type=F32)


def _dot_nt(a, b):
    return lax.dot_general(a, b, (((1,), (1,)), ((), ())), preferred_element_type=F32)


def _split2(x):
    hi = x.astype(BF16)
    lo = (x - hi.astype(F32)).astype(BF16)
    return hi, lo


def _rms(x):
    return x * lax.rsqrt(jnp.mean(x * x, axis=-1, keepdims=True) + EPS)


def _silu(x):
    return x * jax.nn.sigmoid(x)


def _lane_lo_mask(shape):
    lane = lax.broadcasted_iota(jnp.int32, shape, len(shape) - 1)
    return (lane % LANES) < HEAD_DIM


def _ada_kernel(c_ref, w_ref, b_ref, o_ref):
    a = _silu(c_ref[...]).astype(BF16)
    o_ref[...] = _dot(a, w_ref[...].astype(BF16)) + b_ref[...]


def _ada_call(cvec, ada_w, ada_b):
    depth, d, n = ada_w.shape
    rows = cvec.shape[0]
    tn = 1536
    return pl.pallas_call(
        _ada_kernel,
        grid=(depth, n // tn),
        in_specs=[
            pl.BlockSpec((rows, d), lambda l, j: (0, 0)),
            pl.BlockSpec((None, d, tn), lambda l, j: (l, 0, j)),
            pl.BlockSpec((None, 1, tn), lambda l, j: (l, 0, j)),
        ],
        out_specs=pl.BlockSpec((None, rows, tn), lambda l, j: (l, 0, j)),
        out_shape=jax.ShapeDtypeStruct((depth, rows, n), F32),
        compiler_params=_cparams(2),
        name="ada",
    )(cvec, ada_w, ada_b.reshape(depth, 1, n))


def _modulated(h, mods, g, shift_row, scale_row):
    return (_rms(h) * g) * (1.0 + mods[scale_row:scale_row + 1, :]) + mods[shift_row:shift_row + 1, :]


def _rope(x, c, sa, sb):
    outs = []
    for j in range(x.shape[1] // LANES):
        xg = x[:, j * LANES:(j + 1) * LANES]
        outs.append(xg * c + pltpu.roll(xg, 16, 1) * sa + pltpu.roll(xg, LANES - 16, 1) * sb)
    return outs[0] if len(outs) == 1 else jnp.concatenate(outs, axis=1)


def _head_rms(q, bd, g):
    hi, lo = _split2(q * q)
    ss = _dot(hi, bd) + _dot(lo, bd)
    return q * lax.rsqrt(ss * (1.0 / HEAD_DIM) + EPS) * g


def _proj_kernel(h_ref, mods_ref, g_ref, w_ref, bd_ref, qkg_ref, wdec_ref, bdec_ref,
                 c_ref, sa_ref, sb_ref,
                 qa_ref, ka_ref, va_ref, qb_ref, kb_ref, vb_ref, gb_ref, qc_ref, kc_ref, vc_ref,
                 la_ref):
    u = _modulated(h_ref[...], mods_ref[...], g_ref[...], 0, 1).astype(BF16)
    c, sa, sb = c_ref[...], sa_ref[...], sb_ref[...]

    def seg(off, width):
        return _dot(u, w_ref[:, off:off + width])

    scale = HEAD_DIM ** -0.5 * math.log2(math.e)
    qa = _head_rms(seg(OFF_QA, W_QA), bd_ref[...], qkg_ref[0:1, :])
    qa_ref[...] = (_rope(qa, c, sa, sb) * scale).astype(BF16)
    ka = _head_rms(seg(OFF_KA, W_KA), bd_ref[0:W_KA, 0:W_KA], qkg_ref[1:2, 0:W_KA])
    ka_ref[...] = _rope(ka, c, sa, sb).astype(BF16)
    va_ref[...] = seg(OFF_VA, W_VA).astype(BF16)
    qb_ref[...] = seg(OFF_QB, W_QB)
    kb_ref[...] = seg(OFF_KB, W_KB)
    vb_ref[...] = seg(OFF_VB, W_VB).astype(BF16)
    gb_ref[...] = seg(OFF_GB, W_GB)
    qc_ref[...] = (_rope(seg(OFF_QC, W_QC), c, sa, sb) * scale).astype(BF16)
    kc_ref[...] = _rope(seg(OFF_KC, W_KC), c, sa, sb).astype(BF16)
    vc_ref[...] = seg(OFF_VC, W_VC).astype(BF16)
    z = _dot(seg(OFF_RB, W_RB).astype(BF16), wdec_ref[...]) + bdec_ref[...]
    la_ref[...] = (jnp.minimum(z, 0.0) - jnp.log1p(jnp.exp(-jnp.abs(z)))) * (1.0 / GLA_TAU)


def _proj_call(h, mods, g_pre, w_proj, bd, qkg, wdec, bdec, rope_c, rope_sa, rope_sb):
    b, s, d = h.shape
    nblk = s // TOK
    tok = lambda w: pl.BlockSpec((None, TOK, w), lambda bi, i: (bi, i, 0))
    full = lambda a: pl.BlockSpec(a.shape, lambda bi, i: (0,) * a.ndim)
    rope = pl.BlockSpec((TOK, LANES), lambda bi, i: (i, 0))
    widths = [(W_QA, BF16), (W_KA, BF16), (W_VA, BF16), (W_QB, F32), (W_KB, F32), (W_VB, BF16),
              (W_GB, F32), (W_QC, BF16), (W_KC, BF16), (W_VC, BF16), (2 * W_KB, F32)]
    return pl.pallas_call(
        _proj_kernel,
        grid=(b, nblk),
        in_specs=[
            tok(d),
            pl.BlockSpec((None, None, ADA_CHUNKS, d), lambda bi, i: (bi, jnp.minimum(i, 1), 0, 0)),
            full(g_pre), full(w_proj), full(bd), full(qkg), full(wdec), full(bdec),
            rope, rope, rope,
        ],
        out_specs=[tok(w) for w, _ in widths],
        out_shape=[jax.ShapeDtypeStruct((b, s, w), dt) for w, dt in widths],
        compiler_params=_cparams(2),
        name="proj",
    )(h, mods, g_pre, w_proj, bd, qkg, wdec, bdec, rope_c, rope_sa, rope_sb)


KEY_TILE = 256


def _scores(qm, k_ref, grp, nk, s_ref, m_ref):
    pm = None
    for t in range(nk // KEY_TILE):
        ks = slice(t * KEY_TILE, (t + 1) * KEY_TILE)
        st = _dot_nt(qm, k_ref[ks, grp])
        s_ref[:, ks] = st
        for j in range(KEY_TILE // LANES):
            part = st[:, j * LANES:(j + 1) * LANES]
            pm = part if pm is None else jnp.maximum(pm, part)
    m_ref[...] = jnp.broadcast_to(jnp.max(pm, axis=-1, keepdims=True), (TOK, LANES))


def _exp_rows(s_ref, m_ref, e_ref, ps_ref, nk):
    def body(r, carry):
        rows = pl.ds(pl.multiple_of(r * ROW_CHUNK, ROW_CHUNK), ROW_CHUNK)
        mb = m_ref[rows, :]
        acc = None
        for j in range(nk // LANES):
            cols = slice(j * LANES, (j + 1) * LANES)
            e = jnp.exp2(s_ref[rows, cols] - mb)
            acc = e if acc is None else acc + e
            e_ref[rows, cols] = e.astype(e_ref.dtype)
        ps_ref[rows, :] = acc
        return carry
    lax.fori_loop(0, TOK // ROW_CHUNK, body, 0, unroll=2)
    return jnp.sum(ps_ref[...], axis=-1, keepdims=True)


def _diff_rows(e1_ref, e2_ref, r1_ref, r2_ref, p_ref, nk):
    def body(r, carry):
        rows = pl.ds(pl.multiple_of(r * ROW_CHUNK, ROW_CHUNK), ROW_CHUNK)
        a = r1_ref[rows, :]
        b = r2_ref[rows, :]
        for j in range(nk // LANES):
            cols = slice(j * LANES, (j + 1) * LANES)
            p_ref[rows, cols] = (e1_ref[rows, cols] * a - e2_ref[rows, cols] * b).astype(BF16)
        return carry
    lax.fori_loop(0, TOK // ROW_CHUNK, body, 0, unroll=2)


def _attn_body(nk, lambda_init, qa_ref, qc_ref, ka_ref, va_ref, kc_ref, vc_ref, lam_ref, dg_ref,
               ya_ref, yc_ref, s1_ref, s2_ref, p_ref, m_ref, ps_ref, r1_ref, r2_ref):
    lo = _lane_lo_mask((TOK, LANES))
    lo_b = lo.astype(BF16)
    hi_b = 1.0 - lo_b
    half = lambda q_grp, first: q_grp * (lo_b if first else hi_b)
    all_lanes = slice(0, LANES)

    for j in range(A_Q_HEADS // 2):
        q_grp = qa_ref[:, j * LANES:(j + 1) * LANES]
        halves = []
        for first in (True, False):
            _scores(half(q_grp, first), ka_ref, all_lanes, nk, s1_ref, m_ref)
            l = _exp_rows(s1_ref, m_ref, p_ref, ps_ref, nk)
            halves.append(_dot(p_ref[:, 0:nk], va_ref[0:nk, :]) * (1.0 / l))
        ya_ref[:, j * LANES:(j + 1) * LANES] = jnp.where(lo, halves[0], halves[1]).astype(BF16)

    lp = lam_ref[...]
    lam = (jnp.exp(jnp.sum(lp[0:1] * lp[1:2], axis=-1, keepdims=True))
           - jnp.exp(jnp.sum(lp[2:3] * lp[3:4], axis=-1, keepdims=True)) + lambda_init)
    for h in range(C_HEADS):
        grp = slice(h * LANES, (h + 1) * LANES)
        q_grp = qc_ref[:, grp]
        _scores(half(q_grp, True), kc_ref, grp, nk, s1_ref, m_ref)
        l1 = _exp_rows(s1_ref, m_ref, s1_ref, ps_ref, nk)
        r1_ref[...] = jnp.broadcast_to(1.0 / l1, (TOK, LANES))
        _scores(half(q_grp, False), kc_ref, grp, nk, s2_ref, m_ref)
        l2 = _exp_rows(s2_ref, m_ref, s2_ref, ps_ref, nk)
        r2_ref[...] = jnp.broadcast_to(lam / l2, (TOK, LANES))
        _diff_rows(s1_ref, s2_ref, r1_ref, r2_ref, p_ref, nk)
        o = _dot(p_ref[:, 0:nk], vc_ref[0:nk, grp])
        yc_ref[:, grp] = (_rms(o) * dg_ref[...] * (1.0 - lambda_init)).astype(BF16)


def _attn_kernel(n_ctx, first_blk, lambda_init, *refs):
    n_all = refs[2].shape[0]
    if first_blk == 0:
        blk = pl.program_id(1)

        @pl.when(blk == 0)
        def _():
            _attn_body(n_ctx, lambda_init, *refs)

        @pl.when(blk > 0)
        def _():
            _attn_body(n_all, lambda_init, *refs)
    else:
        _attn_body(n_all, lambda_init, *refs)


def _attn_call(qa, qc, ka, va, kc, vc, lam_p, diff_g, *, n_ctx, first_blk, lambda_init):
    b, s, _ = qa.shape
    nblk = s // TOK - first_blk
    tok = pl.BlockSpec((None, TOK, 512), lambda bi, i: (bi, i + first_blk, 0))
    out_tok = pl.BlockSpec((None, TOK, 512), lambda bi, i: (bi, i, 0))
    kv = lambda w: pl.BlockSpec((None, s, w), lambda bi, i: (bi, 0, 0))
    full = lambda a: pl.BlockSpec(a.shape, lambda bi, i: (0,) * a.ndim)
    return pl.pallas_call(
        functools.partial(_attn_kernel, n_ctx, first_blk, lambda_init),
        grid=(b, nblk),
        in_specs=[tok, tok, kv(W_KA), kv(W_VA), kv(W_KC), kv(W_VC), full(lam_p), full(diff_g)],
        out_specs=[out_tok, out_tok],
        out_shape=[jax.ShapeDtypeStruct((b, nblk * TOK, 512), BF16)] * 2,
        scratch_shapes=[
            pltpu.VMEM((TOK, s), F32), pltpu.VMEM((TOK, s), F32), pltpu.VMEM((TOK, s), BF16),
        ] + [pltpu.VMEM((TOK, LANES), F32)] * 4,
        compiler_params=_cparams(2),
        name="attn",
    )(qa, qc, ka, va, kc, vc, lam_p, diff_g)


def _gla_kernel(reverse, final, *refs):
    if final:
        (q_ref, k_ref, v_ref, g_ref, ob_ref, gate_ref, ng_ref, o_ref, state_ref) = refs
    else:
        (q_ref, k_ref, v_ref, g_ref, o_ref, state_ref) = refs
    n_chunks = TOK // GLA_CHUNK
    dkw = B_HEADS * B_DK
    dvw = B_HEADS * B_DV

    @pl.when(pl.program_id(1) == 0)
    def _():
        state_ref[...] = jnp.zeros_like(state_ref)

    row = lax.broadcasted_iota(jnp.int32, (TOK, TOK), 0)
    col = lax.broadcasted_iota(jnp.int32, (TOK, TOK), 1)
    same = (row // GLA_CHUNK) == (col // GLA_CHUNK)
    tri = same & ((col >= row) if reverse else (col <= row))
    tri_b = tri.astype(BF16)
    tri_t = (same & ((row >= col) if reverse else (row <= col))).astype(BF16)
    same_b = same.astype(BF16)

    q = q_ref[...] * (B_DK ** -0.5)
    k = k_ref[...]
    v = v_ref[...]
    g = g_ref[...]
    g_hi, g_lo = _split2(g)
    cum = _dot(tri_b, g_hi) + _dot(tri_b, g_lo)
    q_df = q * jnp.exp(cum)
    q_d = q_df.astype(BF16)
    k_d = (k * jnp.exp(-cum)).astype(BF16)
    gt_hi, gt_lo = _split2(g.T)
    cum_t = _dot(gt_hi, tri_t) + _dot(gt_lo, tri_t)
    last_t = _dot(gt_hi, same_b) + _dot(gt_lo, same_b)
    k_end_t = k.T * jnp.exp(last_t - cum_t)
    sel = ((lax.broadcasted_iota(jnp.int32, (TOK, n_chunks * LANES), 0) // GLA_CHUNK)
           == (lax.broadcasted_iota(jnp.int32, (TOK, n_chunks * LANES), 1) // LANES)).astype(BF16)
    decay = jnp.exp(_dot(gt_hi, sel) + _dot(gt_lo, sel))

    lo = _lane_lo_mask((TOK, LANES))
    o_parts = []
    for h in range(B_HEADS):
        grp = slice((h // 2) * LANES, (h // 2 + 1) * LANES)
        qm = jnp.where(lo if h % 2 == 0 else ~lo, q_df[:, grp], 0.0).astype(BF16)
        att = jnp.where(tri, _dot_nt(qm, k_d[:, grp]), 0.0).astype(BF16)
        o_parts.append(_dot(att, v[:, h * B_DV:(h + 1) * B_DV]))
    o_intra = jnp.concatenate(o_parts, axis=1)

    tok_chunk = lax.broadcasted_iota(jnp.int32, (dkw, TOK), 1) // GLA_CHUNK
    outs = [None] * n_chunks
    order = range(n_chunks - 1, -1, -1) if reverse else range(n_chunks)
    for c in order:
        rows = slice(c * GLA_CHUNK, (c + 1) * GLA_CHUNK)
        outs[c] = o_intra[rows, :] + _dot(q_d[rows, :], state_ref[...].astype(BF16))
        inc = _dot(jnp.where(tok_chunk == c, k_end_t, 0.0).astype(BF16), v)
        dec = decay[:, c * LANES:(c + 1) * LANES]
        for h in range(B_HEADS):
            hr = slice(h * B_DK, (h + 1) * B_DK)
            hc = slice(h * B_DV, (h + 1) * B_DV)
            state_ref[hr, hc] = dec[hr, :] * state_ref[hr, hc] + inc[hr, hc]
    o = jnp.concatenate(outs, axis=0)

    if final:
        o = o + ob_ref[...]
        gate = _silu(gate_ref[...])
        parts = []
        for h in range(B_HEADS):
            hc = slice(h * B_DV, (h + 1) * B_DV)
            parts.append(_rms(o[:, hc]) * ng_ref[...] * gate[:, hc])
        o_ref[...] = jnp.concatenate(parts, axis=1).astype(o_ref.dtype)
    else:
        o_ref[...] = o


def _gla_call(qb, kb, vb, la, extra, *, reverse):
    b, s, _ = qb.shape
    nblk = s // TOK
    final = extra is not None
    if reverse:
        blk = lambda i: jnp.where(i == 0, 0, nblk - i)
    else:
        blk = lambda i: i
    tok = lambda w, cb=0: pl.BlockSpec((None, TOK, w), lambda bi, i: (bi, blk(i), cb))
    in_specs = [tok(W_QB), tok(W_KB), tok(W_VB), tok(W_KB, 1 if reverse else 0)]
    args = [qb, kb, vb, la]
    if final:
        ob, gb, ng = extra
        in_specs += [tok(W_VB), tok(W_GB), pl.BlockSpec(ng.shape, lambda bi, i: (0, 0))]
        args += [ob, gb, ng]
    return pl.pallas_call(
        functools.partial(_gla_kernel, reverse, final),
        grid=(b, nblk),
        in_specs=in_specs,
        out_specs=tok(W_VB),
        out_shape=jax.ShapeDtypeStruct((b, s, W_VB), BF16 if final else F32),
        scratch_shapes=[pltpu.VMEM((B_HEADS * B_DK, B_HEADS * B_DV), F32)],
        compiler_params=_cparams(2),
        name="gla_bwd" if reverse else "gla_fwd",
    )(*args)


def _merge_kernel(h_ref, mods_ref, g_pre_ref, g_post_ref, ya_ref, yb_ref, yc_ref,
                  wg_ref, wbr_ref, wo_ref, o_ref):
    h = h_ref[...]
    mods = mods_ref[...]
    d = h.shape[1]
    u = _modulated(h, mods, g_pre_ref[...], 0, 1).astype(BF16)
    m = None
    for n, y_ref in enumerate((ya_ref, yb_ref, yc_ref)):
        gate = jax.nn.sigmoid(_dot(u, wg_ref[:, n * d:(n + 1) * d]))
        t = gate * _dot(y_ref[...], wbr_ref[n])
        m = t if m is None else m + t
    y = _dot(m.astype(BF16), wo_ref[...])
    o_ref[...] = h + mods[2:3, :] * (_rms(y) * g_post_ref[...])


def _merge_call(h, mods, g_pre, g_post, ya, yb, yc, w_gates, w_br, w_o, *, first_blk):
    b, s, d = h.shape
    nblk = s // TOK - first_blk
    tok = lambda w: pl.BlockSpec((None, TOK, w), lambda bi, i: (bi, i + first_blk, 0))
    tok0 = lambda w: pl.BlockSpec((None, TOK, w), lambda bi, i: (bi, i, 0))
    full = lambda a: pl.BlockSpec(a.shape, lambda bi, i: (0,) * a.ndim)
    return pl.pallas_call(
        _merge_kernel,
        grid=(b, nblk),
        in_specs=[
            tok(d),
            pl.BlockSpec((None, None, ADA_CHUNKS, d),
                         lambda bi, i: (bi, jnp.minimum(i + first_blk, 1), 0, 0)),
            full(g_pre), full(g_post), tok0(512), tok(512), tok0(512),
            full(w_gates), full(w_br), full(w_o),
        ],
        out_specs=tok0(d),
        out_shape=jax.ShapeDtypeStruct((b, nblk * TOK, d), F32),
        compiler_params=_cparams(2),
        name="merge",
    )(h, mods, g_pre, g_post, ya, yb, yc, w_gates, w_br, w_o)


FFN_COLS = 256


def _ffn_kernel(first_blk, last_blk, h_ref, hp_ref, hn_ref, mods_ref, g_pre_ref, g_post_ref,
                wup_ref, cw_ref, cb_ref, wdn_ref, o_ref, acc_ref):
    blk = pl.program_id(1) + first_blk
    mods = mods_ref[...]
    g_pre = g_pre_ref[...]
    h = h_ref[...]
    f = wdn_ref.shape[0]
    prev_ok = blk >= 2
    next_ok = (blk >= 1) & (blk < last_blk)
    v = _modulated(h, mods, g_pre, 3, 4)
    v_prev = jnp.where(prev_ok, _modulated(hp_ref[...], mods, g_pre, 3, 4), 0.0)
    v_next = jnp.where(next_ok, _modulated(hn_ref[...], mods, g_pre, 3, 4), 0.0)
    vb = v.astype(BF16)
    v_ext = jnp.concatenate([v_prev, v, v_next], axis=0).astype(BF16)
    p0 = SUBLANES
    for c in range(f // FFN_COLS):
        cols = slice(c * FFN_COLS, (c + 1) * FFN_COLS)
        gate = _dot(v_ext, wup_ref[:, cols])
        val = _dot(vb, wup_ref[:, f + c * FFN_COLS:f + (c + 1) * FFN_COLS])
        conv = (gate[p0 - 1:p0 - 1 + TOK, :] * cw_ref[0:1, cols]
                + gate[p0:p0 + TOK, :] * cw_ref[1:2, cols]
                + gate[p0 + 1:p0 + 1 + TOK, :] * cw_ref[2:3, cols]) + cb_ref[:, cols]
        act = (_silu(conv) * val).astype(BF16)
        part = _dot(act, wdn_ref[cols, :])
        if c == 0:
            acc_ref[...] = part
        else:
            acc_ref[...] += part
    o_ref[...] = h + mods[5:6, :] * (_rms(acc_ref[...]) * g_post_ref[...])


def _ffn_call(h, mods, g_pre, g_post, w_up, conv_w, conv_b, w_dn, *, first_blk):
    b, s, d = h.shape
    nblk = s // TOK
    sub = TOK // SUBLANES
    n_sub = s // SUBLANES
    tok = pl.BlockSpec((None, TOK, d), lambda bi, i: (bi, i, 0))
    full = lambda a: pl.BlockSpec(a.shape, lambda bi, i: (0,) * a.ndim)
    prev = pl.BlockSpec((None, SUBLANES, d), lambda bi, i: (bi, jnp.maximum(i * sub - 1, 0), 0))
    nxt = pl.BlockSpec((None, SUBLANES, d), lambda bi, i: (bi, jnp.minimum((i + 1) * sub, n_sub - 1), 0))
    return pl.pallas_call(
        functools.partial(_ffn_kernel, first_blk, nblk - 1 + first_blk),
        grid=(b, nblk),
        in_specs=[
            tok, prev, nxt,
            pl.BlockSpec((None, None, ADA_CHUNKS, d),
                         lambda bi, i: (bi, jnp.minimum(i + first_blk, 1), 0, 0)),
            full(g_pre), full(g_post), full(w_up), full(conv_w), full(conv_b), full(w_dn),
        ],
        out_specs=tok,
        out_shape=jax.ShapeDtypeStruct((b, s, d), F32),
        scratch_shapes=[pltpu.VMEM((TOK, d), F32)],
        compiler_params=_cparams(2),
        name="ffn",
    )(h, h, h, mods, g_pre, g_post, w_up, conv_w, conv_b, w_dn)


def _rope_tables(n_ctx, n_lat):
    rows = n_lat // GRID_W
    row = jnp.repeat(jnp.arange(rows, dtype=F32), GRID_W)
    col = jnp.tile(jnp.arange(GRID_W, dtype=F32), rows)
    inv_freq = ROPE_BASE ** (-jnp.arange(N_FREQ, dtype=F32) / N_FREQ)
    ang = jnp.concatenate([row[:, None] * inv_freq, col[:, None] * inv_freq], axis=-1)
    cos, sin = jnp.cos(ang), jnp.sin(ang)
    zeros = jnp.zeros_like(sin[:, :N_FREQ])

    def head(row_part, col_part_a, col_part_b):
        return jnp.concatenate(row_part + col_part_a + col_part_b, axis=-1)

    cr, cc = cos[:, :N_FREQ], cos[:, N_FREQ:]
    sr, sc = sin[:, :N_FREQ], sin[:, N_FREQ:]
    c = jnp.concatenate([cr, cr, cc, cc], axis=-1)
    sa = jnp.concatenate([zeros, sr, zeros, sc], axis=-1)
    sb = jnp.concatenate([-sr, zeros, -sc, zeros], axis=-1)
    pad = lambda t, fill: jnp.concatenate(
        [jnp.full((n_ctx, HEAD_DIM), fill, F32), t], axis=0)
    two = lambda t: jnp.concatenate([t, t], axis=-1)
    return two(pad(c, 1.0)), two(pad(sa, 0.0)), two(pad(sb, 0.0))


def _pair_heads(n_heads):
    half = n_heads // 2
    order = []
    for j in range(half):
        order += [j, half + j]
    return np.concatenate([np.arange(h * HEAD_DIM, (h + 1) * HEAD_DIM) for h in order])


def kernel(x, c, ctx, c_ctx, ada_w, ada_b, norm_g, w_in, qk_norm_a, gla_w_decay, gla_b_decay,
           gla_norm, diff_lambda, diff_norm, w_branch, w_out, w_ffn_in, ffn_conv_w, ffn_conv_b,
           w_ffn_out):
    bsz, n_lat, d = x.shape
    n_ctx = ctx.shape[1]
    depth = ada_w.shape[0]
    assert n_ctx == TOK and n_lat % TOK == 0 and n_lat % GRID_W == 0
    ffn_dim = w_ffn_out.shape[1]
    assert ffn_dim % FFN_COLS == 0

    perm = _pair_heads(A_Q_HEADS)
    sp = np.cumsum([512, 128, 128, 256, 256, 512, 512, 2 * GLA_RANK, 512, 512, 512])
    qa_w, ka_w, va_w, qb_w, kb_w, vb_w, gb_w, rb_w, qc_w, kc_w, vc_w, gt_w = jnp.split(w_in, sp, axis=-1)
    rb_w = jnp.pad(rb_w, ((0, 0), (0, 0), (0, W_RB - 2 * GLA_RANK)))
    w_proj = jnp.concatenate(
        [qa_w[..., perm], ka_w, va_w, qb_w, kb_w, vb_w, gb_w, qc_w, kc_w, vc_w, rb_w], axis=-1).astype(BF16)
    w_gates = gt_w.astype(BF16)
    w_br = jnp.concatenate([w_branch[:, 0:1, perm, :], w_branch[:, 1:]], axis=1).astype(BF16)
    w_o = w_out.astype(BF16)
    w_up = w_ffn_in.astype(BF16)
    w_dn = w_ffn_out.astype(BF16)
    dk_all = B_HEADS * B_DK
    wdec = jnp.zeros((depth, W_RB, 2 * dk_all), F32)
    wdec = wdec.at[:, 0:GLA_RANK, 0:dk_all].set(gla_w_decay[:, 0])
    wdec = wdec.at[:, GLA_RANK:2 * GLA_RANK, dk_all:].set(gla_w_decay[:, 1]).astype(BF16)
    bdec = gla_b_decay.reshape(depth, 1, 2 * dk_all)
    qkg = jnp.tile(qk_norm_a, (1, 1, A_Q_HEADS))
    head_id = np.arange(W_QA) // HEAD_DIM
    bd = jnp.asarray(head_id[:, None] == head_id[None, :], BF16)
    rope_c, rope_sa, rope_sb = _rope_tables(n_ctx, n_lat)

    rows = -(-(bsz + 1) // SUBLANES) * SUBLANES
    cvec = jnp.zeros((rows, d), F32).at[:bsz].set(c).at[bsz].set(c_ctx)
    ada = _ada_call(cvec, ada_w, ada_b)
    lat = ada[:, :bsz].reshape(depth, bsz, 1, ADA_CHUNKS, d)
    ctx_m = jnp.broadcast_to(ada[:, bsz].reshape(depth, 1, 1, ADA_CHUNKS, d), lat.shape)
    mods_all = jnp.concatenate([ctx_m, lat], axis=2)

    h = jnp.concatenate([ctx, x], axis=1)
    for layer in range(depth):
        last = layer == depth - 1
        first_blk = 1 if last else 0
        lambda_init = 0.8 - 0.6 * math.exp(-0.3 * layer)
        mods = mods_all[layer]
        g = norm_g[layer][:, None, :]
        qa, ka, va, qb, kb, vb, gb, qc, kc, vc, la = _proj_call(
            h, mods, g[0], w_proj[layer], bd, qkg[layer], wdec[layer], bdec[layer],
            rope_c, rope_sa, rope_sb)
        ya, yc = _attn_call(qa, qc, ka, va, kc, vc, diff_lambda[layer], diff_norm[layer][None, :],
                            n_ctx=n_ctx, first_blk=first_blk, lambda_init=lambda_init)
        ob = _gla_call(qb, kb, vb, la, None, reverse=True)
        yb = _gla_call(qb, kb, vb, la, (ob, gb, gla_norm[layer][None, :]), reverse=False)
        h_mid = _merge_call(h, mods, g[0], g[1], ya, yb, yc, w_gates[layer], w_br[layer], w_o[layer],
                            first_blk=first_blk)
        h = _ffn_call(h_mid, mods, g[2], g[3], w_up[layer], ffn_conv_w[layer],
                      ffn_conv_b[layer][None, :], w_dn[layer], first_blk=first_blk)
    return h
```

```python
import functools
import math

import numpy as np
import jax
import jax.numpy as jnp
from jax import lax
from jax.experimental import pallas as pl
from jax.experimental.pallas import tpu as pltpu

F32 = jnp.float32
BF16 = jnp.bfloat16

GRID_W = 64
HEAD_DIM = 64
N_FREQ = HEAD_DIM // 4
ROPE_BASE = 10000.0
A_Q_HEADS = 8
A_KV_HEADS = 2
B_HEADS = 4
B_DK = 64
B_DV = 128
GLA_RANK = 16
GLA_TAU = 16.0
GLA_CHUNK = 64
C_HEADS = 4
N_BRANCHES = 3
ADA_CHUNKS = 6
EPS = 1e-6

LANES = 128
SUBLANES = 8
VMEM_LIMIT_BYTES = 56 * 1024 * 1024

TOK = 256
NB = 2
ROW_CHUNK = 16

W_QA, W_KA, W_VA = 512, 128, 128
W_QB, W_KB, W_VB, W_GB = 256, 256, 512, 512
W_QC, W_KC, W_VC = 512, 512, 512
W_RB = LANES
OFF_QA = 0
OFF_KA = OFF_QA + W_QA
OFF_VA = OFF_KA + W_KA
OFF_QB = OFF_VA + W_VA
OFF_KB = OFF_QB + W_QB
OFF_VB = OFF_KB + W_KB
OFF_GB = OFF_VB + W_VB
OFF_QC = OFF_GB + W_GB
OFF_KC = OFF_QC + W_QC
OFF_VC = OFF_KC + W_KC
OFF_RB = OFF_VC + W_VC
W_PROJ = OFF_RB + W_RB


def _cparams(n_axes):
    return pltpu.CompilerParams(
        dimension_semantics=("arbitrary",) * n_axes,
        vmem_limit_bytes=VMEM_LIMIT_BYTES)


def _dot(a, b):
    return jnp.dot(a, b, preferred_element_type=F32)


def _dot_nt(a, b):
    return lax.dot_general(a, b, (((1,), (1,)), ((), ())), preferred_element_type=F32)


def _split2(x):
    hi = x.astype(BF16)
    lo = (x - hi.astype(F32)).astype(BF16)
    return hi, lo


def _stack(parts):
    return parts[0] if len(parts) == 1 else jnp.concatenate(parts, axis=0)


def _rms(x):
    return x * lax.rsqrt(jnp.mean(x * x, axis=-1, keepdims=True) + EPS)


def _silu(x):
    return x * jax.nn.sigmoid(x)


def _lane_lo_mask(shape):
    lane = lax.broadcasted_iota(jnp.int32, shape, len(shape) - 1)
    return (lane % LANES) < HEAD_DIM


def _ada_kernel(c_ref, w_ref, b_ref, o_ref):
    a = _silu(c_ref[...]).astype(BF16)
    o_ref[...] = _dot(a, w_ref[...].astype(BF16)) + b_ref[...]


def _ada_call(cvec, ada_w, ada_b):
    depth, d, n = ada_w.shape
    rows = cvec.shape[0]
    tn = 1536
    return pl.pallas_call(
        _ada_kernel,
        grid=(depth, n // tn),
        in_specs=[
            pl.BlockSpec((rows, d), lambda l, j: (0, 0)),
            pl.BlockSpec((None, d, tn), lambda l, j: (l, 0, j)),
            pl.BlockSpec((None, 1, tn), lambda l, j: (l, 0, j)),
        ],
        out_specs=pl.BlockSpec((None, rows, tn), lambda l, j: (l, 0, j)),
        out_shape=jax.ShapeDtypeStruct((depth, rows, n), F32),
        compiler_params=_cparams(2),
        name="ada",
    )(cvec, ada_w, ada_b.reshape(depth, 1, n))


def _modulated(h, mods, g, shift_row, scale_row):
    return (_rms(h) * g) * (1.0 + mods[scale_row:scale_row + 1, :]) + mods[shift_row:shift_row + 1, :]


def _rope(x, c, sa, sb):
    outs = []
    for j in range(x.shape[1] // LANES):
        xg = x[:, j * LANES:(j + 1) * LANES]
        outs.append(xg * c + pltpu.roll(xg, 16, 1) * sa + pltpu.roll(xg, LANES - 16, 1) * sb)
    return outs[0] if len(outs) == 1 else jnp.concatenate(outs, axis=1)


def _head_rms(q, bd, g):
    hi, lo = _split2(q * q)
    ss = _dot(hi, bd) + _dot(lo, bd)
    return q * lax.rsqrt(ss * (1.0 / HEAD_DIM) + EPS) * g


def _proj_kernel(h_ref, mods_ref, g_ref, w_ref, bd_ref, qkg_ref, wdec_ref, bdec_ref,
                 c_ref, sa_ref, sb_ref,
                 qa_ref, ka_ref, va_ref, qb_ref, kb_ref, vb_ref, gb_ref, qc_ref, kc_ref, vc_ref,
                 la_ref):
    u = _stack([_modulated(h_ref[n], mods_ref[n], g_ref[...], 0, 1) for n in range(NB)]).astype(BF16)
    c, sa, sb = (_stack([t[...]] * NB) for t in (c_ref, sa_ref, sb_ref))

    def seg(off, width):
        return _dot(u, w_ref[:, off:off + width])

    def put(o_ref, val):
        for n in range(NB):
            o_ref[n] = val[n * TOK:(n + 1) * TOK, :].astype(o_ref.dtype)

    scale = HEAD_DIM ** -0.5 * math.log2(math.e)
    qa = _head_rms(seg(OFF_QA, W_QA), bd_ref[...], qkg_ref[0:1, :])
    put(qa_ref, _rope(qa, c, sa, sb) * scale)
    ka = _head_rms(seg(OFF_KA, W_KA), bd_ref[0:W_KA, 0:W_KA], qkg_ref[1:2, 0:W_KA])
    put(ka_ref, _rope(ka, c, sa, sb))
    put(va_ref, seg(OFF_VA, W_VA))
    put(qb_ref, seg(OFF_QB, W_QB))
    put(kb_ref, seg(OFF_KB, W_KB))
    put(vb_ref, seg(OFF_VB, W_VB))
    put(gb_ref, seg(OFF_GB, W_GB))
    put(qc_ref, _rope(seg(OFF_QC, W_QC), c, sa, sb) * scale)
    put(kc_ref, _rope(seg(OFF_KC, W_KC), c, sa, sb))
    put(vc_ref, seg(OFF_VC, W_VC))
    z = _dot(seg(OFF_RB, W_RB).astype(BF16), wdec_ref[...]) + bdec_ref[...]
    put(la_ref, (jnp.minimum(z, 0.0) - jnp.log1p(jnp.exp(-jnp.abs(z)))) * (1.0 / GLA_TAU))


def _proj_call(h, mods, g_pre, w_proj, bd, qkg, wdec, bdec, rope_c, rope_sa, rope_sb):
    b, s, d = h.shape
    nblk = s // TOK
    tok = lambda w: pl.BlockSpec((NB, TOK, w), lambda bi, i: (bi, i, 0))
    full = lambda a: pl.BlockSpec(a.shape, lambda bi, i: (0,) * a.ndim)
    rope = pl.BlockSpec((TOK, LANES), lambda bi, i: (i, 0))
    widths = [(W_QA, BF16), (W_KA, BF16), (W_VA, BF16), (W_QB, F32), (W_KB, F32), (W_VB, BF16),
              (W_GB, F32), (W_QC, BF16), (W_KC, BF16), (W_VC, BF16), (2 * W_KB, F32)]
    return pl.pallas_call(
        _proj_kernel,
        grid=(b // NB, nblk),
        in_specs=[
            tok(d),
            pl.BlockSpec((NB, None, ADA_CHUNKS, d), lambda bi, i: (bi, jnp.minimum(i, 1), 0, 0)),
            full(g_pre), full(w_proj), full(bd), full(qkg), full(wdec), full(bdec),
            rope, rope, rope,
        ],
        out_specs=[tok(w) for w, _ in widths],
        out_shape=[jax.ShapeDtypeStruct((b, s, w), dt) for w, dt in widths],
        compiler_params=_cparams(2),
        name="proj",
    )(h, mods, g_pre, w_proj, bd, qkg, wdec, bdec, rope_c, rope_sa, rope_sb)


KEY_TILE = 256
N_SBUF = 4
N_PBUF = 2


def _scores(qm, k_ref, grp, nk, s_ref, m_ref):
    pm = None
    for t in range(nk // KEY_TILE):
        ks = slice(t * KEY_TILE, (t + 1) * KEY_TILE)
        st = _dot_nt(qm, k_ref[ks, grp])
        s_ref[:, ks] = st
        for j in range(KEY_TILE // LANES):
            part = st[:, j * LANES:(j + 1) * LANES]
            pm = part if pm is None else jnp.maximum(pm, part)
    m_ref[...] = jnp.broadcast_to(jnp.max(pm, axis=-1, keepdims=True), (TOK, LANES))


def _exp_rows(s_ref, m_ref, e_ref, ps_ref, nk):
    for r in range(TOK // ROW_CHUNK):
        rows = slice(r * ROW_CHUNK, (r + 1) * ROW_CHUNK)
        mb = m_ref[rows, :]
        acc = None
        for j in range(nk // LANES):
            cols = slice(j * LANES, (j + 1) * LANES)
            e = jnp.exp2(s_ref[rows, cols] - mb)
            acc = e if acc is None else acc + e
            e_ref[rows, cols] = e.astype(e_ref.dtype)
        ps_ref[rows, :] = acc
    return jnp.sum(ps_ref[...], axis=-1, keepdims=True)


def _diff_rows(e1_ref, e2_ref, r1_ref, r2_ref, p_ref, nk):
    for r in range(TOK // ROW_CHUNK):
        rows = slice(r * ROW_CHUNK, (r + 1) * ROW_CHUNK)
        a = r1_ref[rows, :]
        b = r2_ref[rows, :]
        for j in range(nk // LANES):
            cols = slice(j * LANES, (j + 1) * LANES)
            p_ref[rows, cols] = (e1_ref[rows, cols] * a - e2_ref[rows, cols] * b).astype(BF16)


def _attn_body(nk, lambda_init, qa_ref, qc_ref, ka_ref, va_ref, kc_ref, vc_ref, lam_ref, dg_ref,
               ya_ref, yc_ref, *scratch):
    s_refs, p_refs = scratch[0:N_SBUF], scratch[N_SBUF:N_SBUF + N_PBUF]
    small = scratch[N_SBUF + N_PBUF:]
    m_refs, ps_refs, r_refs = small[0:N_SBUF], small[N_SBUF:2 * N_SBUF], small[2 * N_SBUF:3 * N_SBUF]
    lo = _lane_lo_mask((TOK, LANES))
    lo_b = lo.astype(BF16)
    hi_b = 1.0 - lo_b
    half = lambda q_grp, first: q_grp * (lo_b if first else hi_b)
    all_lanes = slice(0, LANES)

    unit = 0
    for j in range(A_Q_HEADS // 2):
        q_grp = qa_ref[:, j * LANES:(j + 1) * LANES]
        halves = []
        for first in (True, False):
            sb, pb = unit % N_SBUF, unit % N_PBUF
            _scores(half(q_grp, first), ka_ref, all_lanes, nk, s_refs[sb], m_refs[sb])
            l = _exp_rows(s_refs[sb], m_refs[sb], p_refs[pb], ps_refs[sb], nk)
            halves.append(_dot(p_refs[pb][:, 0:nk], va_ref[0:nk, :]) * (1.0 / l))
            unit += 1
        ya_ref[:, j * LANES:(j + 1) * LANES] = jnp.where(lo, halves[0], halves[1]).astype(BF16)

    lp = lam_ref[...]
    lam = (jnp.exp(jnp.sum(lp[0:1] * lp[1:2], axis=-1, keepdims=True))
           - jnp.exp(jnp.sum(lp[2:3] * lp[3:4], axis=-1, keepdims=True)) + lambda_init)
    for h in range(C_HEADS):
        grp = slice(h * LANES, (h + 1) * LANES)
        q_grp = qc_ref[:, grp]
        b1, b2, pb = unit % N_SBUF, (unit + 1) % N_SBUF, (unit // 2) % N_PBUF
        _scores(half(q_grp, True), kc_ref, grp, nk, s_refs[b1], m_refs[b1])
        l1 = _exp_rows(s_refs[b1], m_refs[b1], s_refs[b1], ps_refs[b1], nk)
        r_refs[b1][...] = jnp.broadcast_to(1.0 / l1, (TOK, LANES))
        _scores(half(q_grp, False), kc_ref, grp, nk, s_refs[b2], m_refs[b2])
        l2 = _exp_rows(s_refs[b2], m_refs[b2], s_refs[b2], ps_refs[b2], nk)
        r_refs[b2][...] = jnp.broadcast_to(lam / l2, (TOK, LANES))
        _diff_rows(s_refs[b1], s_refs[b2], r_refs[b1], r_refs[b2], p_refs[pb], nk)
        o = _dot(p_refs[pb][:, 0:nk], vc_ref[0:nk, grp])
        yc_ref[:, grp] = (_rms(o) * dg_ref[...] * (1.0 - lambda_init)).astype(BF16)
        unit += 2


def _attn_kernel(n_ctx, first_blk, lambda_init, *refs):
    n_all = refs[2].shape[0]
    if first_blk == 0:
        blk = pl.program_id(1)

        @pl.when(blk == 0)
        def _():
            _attn_body(n_ctx, lambda_init, *refs)

        @pl.when(blk > 0)
        def _():
            _attn_body(n_all, lambda_init, *refs)
    else:
        _attn_body(n_all, lambda_init, *refs)


def _attn_call(qa, qc, ka, va, kc, vc, lam_p, diff_g, *, n_ctx, first_blk, lambda_init):
    b, s, _ = qa.shape
    nblk = s // TOK - first_blk
    tok = pl.BlockSpec((None, TOK, 512), lambda bi, i: (bi, i + first_blk, 0))
    out_tok = pl.BlockSpec((None, TOK, 512), lambda bi, i: (bi, i, 0))
    kv = lambda w: pl.BlockSpec((None, s, w), lambda bi, i: (bi, 0, 0))
    full = lambda a: pl.BlockSpec(a.shape, lambda bi, i: (0,) * a.ndim)
    return pl.pallas_call(
        functools.partial(_attn_kernel, n_ctx, first_blk, lambda_init),
        grid=(b, nblk),
        in_specs=[tok, tok, kv(W_KA), kv(W_VA), kv(W_KC), kv(W_VC), full(lam_p), full(diff_g)],
        out_specs=[out_tok, out_tok],
        out_shape=[jax.ShapeDtypeStruct((b, nblk * TOK, 512), BF16)] * 2,
        scratch_shapes=(
            [pltpu.VMEM((TOK, s), F32)] * N_SBUF + [pltpu.VMEM((TOK, s), BF16)] * N_PBUF
            + [pltpu.VMEM((TOK, LANES), F32)] * (3 * N_SBUF)),
        compiler_params=_cparams(2),
        name="attn",
    )(qa, qc, ka, va, kc, vc, lam_p, diff_g)


def _gla_kernel(reverse, final, *refs):
    if final:
        (q_ref, k_ref, v_ref, g_ref, ob_ref, gate_ref, ng_ref, o_ref, state_ref) = refs
    else:
        (q_ref, k_ref, v_ref, g_ref, o_ref, state_ref) = refs
    n_chunks = TOK // GLA_CHUNK
    dkw = B_HEADS * B_DK
    dvw = B_HEADS * B_DV

    @pl.when(pl.program_id(1) == 0)
    def _():
        state_ref[...] = jnp.zeros_like(state_ref)

    row = lax.broadcasted_iota(jnp.int32, (TOK, TOK), 0)
    col = lax.broadcasted_iota(jnp.int32, (TOK, TOK), 1)
    same = (row // GLA_CHUNK) == (col // GLA_CHUNK)
    tri = same & ((col >= row) if reverse else (col <= row))
    tri_b = tri.astype(BF16)
    tri_t = (same & ((row >= col) if reverse else (row <= col))).astype(BF16)
    same_b = same.astype(BF16)

    q = q_ref[...] * (B_DK ** -0.5)
    k = k_ref[...]
    v = v_ref[...]
    g = g_ref[...]
    g_hi, g_lo = _split2(g)
    cum = _dot(tri_b, g_hi) + _dot(tri_b, g_lo)
    q_df = q * jnp.exp(cum)
    q_d = q_df.astype(BF16)
    k_d = (k * jnp.exp(-cum)).astype(BF16)
    gt_hi, gt_lo = _split2(g.T)
    cum_t = _dot(gt_hi, tri_t) + _dot(gt_lo, tri_t)
    last_t = _dot(gt_hi, same_b) + _dot(gt_lo, same_b)
    k_end_t = k.T * jnp.exp(last_t - cum_t)
    sel = ((lax.broadcasted_iota(jnp.int32, (TOK, n_chunks * LANES), 0) // GLA_CHUNK)
           == (lax.broadcasted_iota(jnp.int32, (TOK, n_chunks * LANES), 1) // LANES)).astype(BF16)
    decay = jnp.exp(_dot(gt_hi, sel) + _dot(gt_lo, sel))

    lo = _lane_lo_mask((TOK, LANES))
    o_parts = []
    for h in range(B_HEADS):
        grp = slice((h // 2) * LANES, (h // 2 + 1) * LANES)
        qm = jnp.where(lo if h % 2 == 0 else ~lo, q_df[:, grp], 0.0).astype(BF16)
        att = jnp.where(tri, _dot_nt(qm, k_d[:, grp]), 0.0).astype(BF16)
        o_parts.append(_dot(att, v[:, h * B_DV:(h + 1) * B_DV]))
    o_intra = jnp.concatenate(o_parts, axis=1)

    tok_chunk = lax.broadcasted_iota(jnp.int32, (dkw, TOK), 1) // GLA_CHUNK
    outs = [None] * n_chunks
    order = range(n_chunks - 1, -1, -1) if reverse else range(n_chunks)
    for c in order:
        rows = slice(c * GLA_CHUNK, (c + 1) * GLA_CHUNK)
        outs[c] = o_intra[rows, :] + _dot(q_d[rows, :], state_ref[...].astype(BF16))
        inc = _dot(jnp.where(tok_chunk == c, k_end_t, 0.0).astype(BF16), v)
        dec = decay[:, c * LANES:(c + 1) * LANES]
        for h in range(B_HEADS):
            hr = slice(h * B_DK, (h + 1) * B_DK)
            hc = slice(h * B_DV, (h + 1) * B_DV)
            state_ref[hr, hc] = dec[hr, :] * state_ref[hr, hc] + inc[hr, hc]
    o = jnp.concatenate(outs, axis=0)

    if final:
        o = o + ob_ref[...]
        gate = _silu(gate_ref[...])
        parts = []
        for h in range(B_HEADS):
            hc = slice(h * B_DV, (h + 1) * B_DV)
            parts.append(_rms(o[:, hc]) * ng_ref[...] * gate[:, hc])
        o_ref[...] = jnp.concatenate(parts, axis=1).astype(o_ref.dtype)
    else:
        o_ref[...] = o


def _gla_call(qb, kb, vb, la, extra, *, reverse):
    b, s, _ = qb.shape
    nblk = s // TOK
    final = extra is not None
    if reverse:
        blk = lambda i: jnp.where(i == 0, 0, nblk - i)
    else:
        blk = lambda i: i
    tok = lambda w, cb=0: pl.BlockSpec((None, TOK, w), lambda bi, i: (bi, blk(i), cb))
    in_specs = [tok(W_QB), tok(W_KB), tok(W_VB), tok(W_KB, 1 if reverse else 0)]
    args = [qb, kb, vb, la]
    if final:
        ob, gb, ng = extra
        in_specs += [tok(W_VB), tok(W_GB), pl.BlockSpec(ng.shape, lambda bi, i: (0, 0))]
        args += [ob, gb, ng]
    return pl.pallas_call(
        functools.partial(_gla_kernel, reverse, final),
        grid=(b, nblk),
        in_specs=in_specs,
        out_specs=tok(W_VB),
        out_shape=jax.ShapeDtypeStruct((b, s, W_VB), BF16 if final else F32),
        scratch_shapes=[pltpu.VMEM((B_HEADS * B_DK, B_HEADS * B_DV), F32)],
        compiler_params=_cparams(2),
        name="gla_bwd" if reverse else "gla_fwd",
    )(*args)


def _merge_kernel(h_ref, mods_ref, g_pre_ref, g_post_ref, ya_ref, yb_ref, yc_ref,
                  wg_ref, wbr_ref, wo_ref, o_ref):
    d = h_ref.shape[2]
    u = _stack([_modulated(h_ref[n], mods_ref[n], g_pre_ref[...], 0, 1) for n in range(NB)]).astype(BF16)
    m = None
    for br, y_ref in enumerate((ya_ref, yb_ref, yc_ref)):
        gate = jax.nn.sigmoid(_dot(u, wg_ref[:, br * d:(br + 1) * d]))
        t = gate * _dot(_stack([y_ref[n] for n in range(NB)]), wbr_ref[br])
        m = t if m is None else m + t
    y = _rms(_dot(m.astype(BF16), wo_ref[...])) * g_post_ref[...]
    for n in range(NB):
        o_ref[n] = h_ref[n] + mods_ref[n][2:3, :] * y[n * TOK:(n + 1) * TOK, :]


def _merge_call(h, mods, g_pre, g_post, ya, yb, yc, w_gates, w_br, w_o, *, first_blk):
    b, s, d = h.shape
    nblk = s // TOK - first_blk
    tok = lambda w: pl.BlockSpec((NB, TOK, w), lambda bi, i: (bi, i + first_blk, 0))
    tok0 = lambda w: pl.BlockSpec((NB, TOK, w), lambda bi, i: (bi, i, 0))
    full = lambda a: pl.BlockSpec(a.shape, lambda bi, i: (0,) * a.ndim)
    return pl.pallas_call(
        _merge_kernel,
        grid=(b // NB, nblk),
        in_specs=[
            tok(d),
            pl.BlockSpec((NB, None, ADA_CHUNKS, d),
                         lambda bi, i: (bi, jnp.minimum(i + first_blk, 1), 0, 0)),
            full(g_pre), full(g_post), tok0(512), tok(512), tok0(512),
            full(w_gates), full(w_br), full(w_o),
        ],
        out_specs=tok0(d),
        out_shape=jax.ShapeDtypeStruct((b, nblk * TOK, d), F32),
        compiler_params=_cparams(2),
        name="merge",
    )(h, mods, g_pre, g_post, ya, yb, yc, w_gates, w_br, w_o)


FFN_COLS = 256


def _ffn_kernel(first_blk, last_blk, h_ref, hp_ref, hn_ref, mods_ref, g_pre_ref, g_post_ref,
                wup_ref, cw_ref, cb_ref, wdn_ref, o_ref, acc_ref):
    blk = pl.program_id(1) + first_blk
    g_pre = g_pre_ref[...]
    f = wdn_ref.shape[0]
    prev_ok = blk >= 2
    next_ok = (blk >= 1) & (blk < last_blk)
    ext = TOK + 2 * SUBLANES
    v_parts, v_ext_parts = [], []
    for n in range(NB):
        mods = mods_ref[n]
        v = _modulated(h_ref[n], mods, g_pre, 3, 4)
        v_prev = jnp.where(prev_ok, _modulated(hp_ref[n], mods, g_pre, 3, 4), 0.0)
        v_next = jnp.where(next_ok, _modulated(hn_ref[n], mods, g_pre, 3, 4), 0.0)
        v_parts.append(v)
        v_ext_parts += [v_prev, v, v_next]
    vb = _stack(v_parts).astype(BF16)
    v_ext = _stack(v_ext_parts).astype(BF16)
    for c in range(f // FFN_COLS):
        cols = slice(c * FFN_COLS, (c + 1) * FFN_COLS)
        gate = _dot(v_ext, wup_ref[:, cols])
        val = _dot(vb, wup_ref[:, f + c * FFN_COLS:f + (c + 1) * FFN_COLS])
        conv_parts = []
        for n in range(NB):
            p0 = n * ext + SUBLANES
            conv_parts.append(gate[p0 - 1:p0 - 1 + TOK, :] * cw_ref[0:1, cols]
                              + gate[p0:p0 + TOK, :] * cw_ref[1:2, cols]
                              + gate[p0 + 1:p0 + 1 + TOK, :] * cw_ref[2:3, cols])
        conv = _stack(conv_parts) + cb_ref[:, cols]
        act = (_silu(conv) * val).astype(BF16)
        part = _dot(act, wdn_ref[cols, :])
        if c == 0:
            acc_ref[...] = part
        else:
            acc_ref[...] += part
    y = _rms(acc_ref[...]) * g_post_ref[...]
    for n in range(NB):
        o_ref[n] = h_ref[n] + mods_ref[n][5:6, :] * y[n * TOK:(n + 1) * TOK, :]


def _ffn_call(h, mods, g_pre, g_post, w_up, conv_w, conv_b, w_dn, *, first_blk):
    b, s, d = h.shape
    nblk = s // TOK
    sub = TOK // SUBLANES
    n_sub = s // SUBLANES
    tok = pl.BlockSpec((NB, TOK, d), lambda bi, i: (bi, i, 0))
    full = lambda a: pl.BlockSpec(a.shape, lambda bi, i: (0,) * a.ndim)
    once = lambda a: pl.BlockSpec(a.shape, lambda bi, i: (0,) * a.ndim, pipeline_mode=pl.Buffered(1))
    prev = pl.BlockSpec((NB, SUBLANES, d), lambda bi, i: (bi, jnp.maximum(i * sub - 1, 0), 0))
    nxt = pl.BlockSpec((NB, SUBLANES, d), lambda bi, i: (bi, jnp.minimum((i + 1) * sub, n_sub - 1), 0))
    return pl.pallas_call(
        functools.partial(_ffn_kernel, first_blk, nblk - 1 + first_blk),
        grid=(b // NB, nblk),
        in_specs=[
            tok, prev, nxt,
            pl.BlockSpec((NB, None, ADA_CHUNKS, d),
                         lambda bi, i: (bi, jnp.minimum(i + first_blk, 1), 0, 0)),
            full(g_pre), full(g_post), once(w_up), full(conv_w), full(conv_b), once(w_dn),
        ],
        out_specs=tok,
        out_shape=jax.ShapeDtypeStruct((b, s, d), F32),
        scratch_shapes=[pltpu.VMEM((NB * TOK, d), F32)],
        compiler_params=_cparams(2),
        name="ffn",
    )(h, h, h, mods, g_pre, g_post, w_up, conv_w, conv_b, w_dn)


def _rope_tables(n_ctx, n_lat):
    rows = n_lat // GRID_W
    row = jnp.repeat(jnp.arange(rows, dtype=F32), GRID_W)
    col = jnp.tile(jnp.arange(GRID_W, dtype=F32), rows)
    inv_freq = ROPE_BASE ** (-jnp.arange(N_FREQ, dtype=F32) / N_FREQ)
    ang = jnp.concatenate([row[:, None] * inv_freq, col[:, None] * inv_freq], axis=-1)
    cos, sin = jnp.cos(ang), jnp.sin(ang)
    zeros = jnp.zeros_like(sin[:, :N_FREQ])

    def head(row_part, col_part_a, col_part_b):
        return jnp.concatenate(row_part + col_part_a + col_part_b, axis=-1)

    cr, cc = cos[:, :N_FREQ], cos[:, N_FREQ:]
    sr, sc = sin[:, :N_FREQ], sin[:, N_FREQ:]
    c = jnp.concatenate([cr, cr, cc, cc], axis=-1)
    sa = jnp.concatenate([zeros, sr, zeros, sc], axis=-1)
    sb = jnp.concatenate([-sr, zeros, -sc, zeros], axis=-1)
    pad = lambda t, fill: jnp.concatenate(
        [jnp.full((n_ctx, HEAD_DIM), fill, F32), t], axis=0)
    two = lambda t: jnp.concatenate([t, t], axis=-1)
    return two(pad(c, 1.0)), two(pad(sa, 0.0)), two(pad(sb, 0.0))


def _pair_heads(n_heads):
    half = n_heads // 2
    order = []
    for j in range(half):
        order += [j, half + j]
    return np.concatenate([np.arange(h * HEAD_DIM, (h + 1) * HEAD_DIM) for h in order])


def kernel(x, c, ctx, c_ctx, ada_w, ada_b, norm_g, w_in, qk_norm_a, gla_w_decay, gla_b_decay,
           gla_norm, diff_lambda, diff_norm, w_branch, w_out, w_ffn_in, ffn_conv_w, ffn_conv_b,
           w_ffn_out):
    bsz, n_lat, d = x.shape
    n_ctx = ctx.shape[1]
    depth = ada_w.shape[0]
    assert n_ctx == TOK and n_lat % TOK == 0 and n_lat % GRID_W == 0 and bsz % NB == 0
    ffn_dim = w_ffn_out.shape[1]
    assert ffn_dim % FFN_COLS == 0

    perm = _pair_heads(A_Q_HEADS)
    sp = np.cumsum([512, 128, 128, 256, 256, 512, 512, 2 * GLA_RANK, 512, 512, 512])
    qa_w, ka_w, va_w, qb_w, kb_w, vb_w, gb_w, rb_w, qc_w, kc_w, vc_w, gt_w = jnp.split(w_in, sp, axis=-1)
    rb_w = jnp.pad(rb_w, ((0, 0), (0, 0), (0, W_RB - 2 * GLA_RANK)))
    w_proj = jnp.concatenate(
        [qa_w[..., perm], ka_w, va_w, qb_w, kb_w, vb_w, gb_w, qc_w, kc_w, vc_w, rb_w], axis=-1).astype(BF16)
    w_gates = gt_w.astype(BF16)
    w_br = jnp.concatenate([w_branch[:, 0:1, perm, :], w_branch[:, 1:]], axis=1).astype(BF16)
    w_o = w_out.astype(BF16)
    w_up = w_ffn_in.astype(BF16)
    w_dn = w_ffn_out.astype(BF16)
    dk_all = B_HEADS * B_DK
    wdec = jnp.zeros((depth, W_RB, 2 * dk_all), F32)
    wdec = wdec.at[:, 0:GLA_RANK, 0:dk_all].set(gla_w_decay[:, 0])
    wdec = wdec.at[:, GLA_RANK:2 * GLA_RANK, dk_all:].set(gla_w_decay[:, 1]).astype(BF16)
    bdec = gla_b_decay.reshape(depth, 1, 2 * dk_all)
    qkg = jnp.tile(qk_norm_a, (1, 1, A_Q_HEADS))
    head_id = np.arange(W_QA) // HEAD_DIM
    bd = jnp.asarray(head_id[:, None] == head_id[None, :], BF16)
    rope_c, rope_sa, rope_sb = _rope_tables(n_ctx, n_lat)

    rows = -(-(bsz + 1) // SUBLANES) * SUBLANES
    cvec = jnp.zeros((rows, d), F32).at[:bsz].set(c).at[bsz].set(c_ctx)
    ada = _ada_call(cvec, ada_w, ada_b)
    lat = ada[:, :bsz].reshape(depth, bsz, 1, ADA_CHUNKS, d)
    ctx_m = jnp.broadcast_to(ada[:, bsz].reshape(depth, 1, 1, ADA_CHUNKS, d), lat.shape)
    mods_all = jnp.concatenate([ctx_m, lat], axis=2)

    h = jnp.concatenate([ctx, x], axis=1)
    for layer in range(depth):
        last = layer == depth - 1
        first_blk = 1 if last else 0
        lambda_init = 0.8 - 0.6 * math.exp(-0.3 * layer)
        mods = mods_all[layer]
        g = norm_g[layer][:, None, :]
        qa, ka, va, qb, kb, vb, gb, qc, kc, vc, la = _proj_call(
            h, mods, g[0], w_proj[layer], bd, qkg[layer], wdec[layer], bdec[layer],
            rope_c, rope_sa, rope_sb)
        ya, yc = _attn_call(qa, qc, ka, va, kc, vc, diff_lambda[layer], diff_norm[layer][None, :],
                            n_ctx=n_ctx, first_blk=first_blk, lambda_init=lambda_init)
        ob = _gla_call(qb, kb, vb, la, None, reverse=True)
        yb = _gla_call(qb, kb, vb, la, (ob, gb, gla_norm[layer][None, :]), reverse=False)
        h_mid = _merge_call(h, mods, g[0], g[1], ya, yb, yc, w_gates[layer], w_br[layer], w_o[layer],
                            first_blk=first_blk)
        h = _ffn_call(h_mid, mods, g[2], g[3], w_up[layer], ffn_conv_w[layer],
                      ffn_conv_b[layer][None, :], w_dn[layer], first_blk=first_blk)
    return h
```

```python
import functools
import math

import numpy as np
import jax
import jax.numpy as jnp
from jax import lax
from jax.experimental import pallas as pl
from jax.experimental.pallas import tpu as pltpu

F32 = jnp.float32
BF16 = jnp.bfloat16

GRID_W = 64
HEAD_DIM = 64
N_FREQ = HEAD_DIM // 4
ROPE_BASE = 10000.0
A_Q_HEADS = 8
A_KV_HEADS = 2
B_HEADS = 4
B_DK = 64
B_DV = 128
GLA_RANK = 16
GLA_TAU = 16.0
GLA_CHUNK = 64
C_HEADS = 4
N_BRANCHES = 3
ADA_CHUNKS = 6
EPS = 1e-6

LANES = 128
SUBLANES = 8
VMEM_LIMIT_BYTES = 56 * 1024 * 1024

TOK = 256
NB = 4
ROW_CHUNK = 16

W_QA, W_KA, W_VA = 512, 128, 128
W_QB, W_KB, W_VB, W_GB = 256, 256, 512, 512
W_QC, W_KC, W_VC = 512, 512, 512
W_RB = LANES
OFF_QA = 0
OFF_KA = OFF_QA + W_QA
OFF_VA = OFF_KA + W_KA
OFF_QB = OFF_VA + W_VA
OFF_KB = OFF_QB + W_QB
OFF_VB = OFF_KB + W_KB
OFF_GB = OFF_VB + W_VB
OFF_QC = OFF_GB + W_GB
OFF_KC = OFF_QC + W_QC
OFF_VC = OFF_KC + W_KC
OFF_RB = OFF_VC + W_VC
W_PROJ = OFF_RB + W_RB


def _cparams(n_axes):
    return pltpu.CompilerParams(
        dimension_semantics=("arbitrary",) * n_axes,
        vmem_limit_bytes=VMEM_LIMIT_BYTES)


def _dot(a, b):
    return jnp.dot(a, b, preferred_element_type=F32)


def _dot_nt(a, b):
    return lax.dot_general(a, b, (((1,), (1,)), ((), ())), preferred_element_type=F32)


def _split2(x):
    hi = x.astype(BF16)
    lo = (x - hi.astype(F32)).astype(BF16)
    return hi, lo


def _stack(parts):
    return parts[0] if len(parts) == 1 else jnp.concatenate(parts, axis=0)


def _rms(x):
    return x * lax.rsqrt(jnp.mean(x * x, axis=-1, keepdims=True) + EPS)


def _silu(x):
    return x * jax.nn.sigmoid(x)


def _lane_lo_mask(shape):
    lane = lax.broadcasted_iota(jnp.int32, shape, len(shape) - 1)
    return (lane % LANES) < HEAD_DIM


def _ada_kernel(c_ref, w_ref, b_ref, o_ref):
    a = _silu(c_ref[...]).astype(BF16)
    o_ref[...] = _dot(a, w_ref[...].astype(BF16)) + b_ref[...]


def _ada_call(cvec, ada_w, ada_b):
    depth, d, n = ada_w.shape
    rows = cvec.shape[0]
    tn = 1536
    return pl.pallas_call(
        _ada_kernel,
        grid=(depth, n // tn),
        in_specs=[
            pl.BlockSpec((rows, d), lambda l, j: (0, 0)),
            pl.BlockSpec((None, d, tn), lambda l, j: (l, 0, j)),
            pl.BlockSpec((None, 1, tn), lambda l, j: (l, 0, j)),
        ],
        out_specs=pl.BlockSpec((None, rows, tn), lambda l, j: (l, 0, j)),
        out_shape=jax.ShapeDtypeStruct((depth, rows, n), F32),
        compiler_params=_cparams(2),
        name="ada",
    )(cvec, ada_w, ada_b.reshape(depth, 1, n))


def _modulated(h, mods, g, shift_row, scale_row):
    return (_rms(h) * g) * (1.0 + mods[scale_row:scale_row + 1, :]) + mods[shift_row:shift_row + 1, :]


def _rope(x, c, sa, sb):
    outs = []
    for j in range(x.shape[1] // LANES):
        xg = x[:, j * LANES:(j + 1) * LANES]
        outs.append(xg * c + pltpu.roll(xg, 16, 1) * sa + pltpu.roll(xg, LANES - 16, 1) * sb)
    return outs[0] if len(outs) == 1 else jnp.concatenate(outs, axis=1)


def _head_rms(q, bd, g):
    hi, lo = _split2(q * q)
    ss = _dot(hi, bd) + _dot(lo, bd)
    return q * lax.rsqrt(ss * (1.0 / HEAD_DIM) + EPS) * g


def _proj_kernel(h_ref, mods_ref, g_ref, w_ref, bd_ref, qkg_ref, wdec_ref, bdec_ref,
                 c_ref, sa_ref, sb_ref,
                 qa_ref, ka_ref, va_ref, qb_ref, kb_ref, vb_ref, gb_ref, qc_ref, kc_ref, vc_ref,
                 la_ref):
    u = _stack([_modulated(h_ref[n], mods_ref[n], g_ref[...], 0, 1) for n in range(NB)]).astype(BF16)
    c, sa, sb = (_stack([t[...]] * NB) for t in (c_ref, sa_ref, sb_ref))

    def seg(off, width):
        return _dot(u, w_ref[:, off:off + width])

    def put(o_ref, val):
        for n in range(NB):
            o_ref[n] = val[n * TOK:(n + 1) * TOK, :].astype(o_ref.dtype)

    scale = HEAD_DIM ** -0.5 * math.log2(math.e)
    qa = _head_rms(seg(OFF_QA, W_QA), bd_ref[...], qkg_ref[0:1, :])
    put(qa_ref, _rope(qa, c, sa, sb) * scale)
    ka = _head_rms(seg(OFF_KA, W_KA), bd_ref[0:W_KA, 0:W_KA], qkg_ref[1:2, 0:W_KA])
    put(ka_ref, _rope(ka, c, sa, sb))
    put(va_ref, seg(OFF_VA, W_VA))
    put(qb_ref, seg(OFF_QB, W_QB))
    put(kb_ref, seg(OFF_KB, W_KB))
    put(vb_ref, seg(OFF_VB, W_VB))
    put(gb_ref, seg(OFF_GB, W_GB))
    put(qc_ref, _rope(seg(OFF_QC, W_QC), c, sa, sb) * scale)
    put(kc_ref, _rope(seg(OFF_KC, W_KC), c, sa, sb))
    put(vc_ref, seg(OFF_VC, W_VC))
    z = _dot(seg(OFF_RB, W_RB).astype(BF16), wdec_ref[...]) + bdec_ref[...]
    put(la_ref, (jnp.minimum(z, 0.0) - jnp.log1p(jnp.exp(-jnp.abs(z)))) * (1.0 / GLA_TAU))


def _proj_call(h, mods, g_pre, w_proj, bd, qkg, wdec, bdec, rope_c, rope_sa, rope_sb):
    b, s, d = h.shape
    nblk = s // TOK
    tok = lambda w: pl.BlockSpec((NB, TOK, w), lambda bi, i: (bi, i, 0))
    full = lambda a: pl.BlockSpec(a.shape, lambda bi, i: (0,) * a.ndim)
    rope = pl.BlockSpec((TOK, LANES), lambda bi, i: (i, 0))
    widths = [(W_QA, BF16), (W_KA, BF16), (W_VA, BF16), (W_QB, F32), (W_KB, F32), (W_VB, BF16),
              (W_GB, F32), (W_QC, BF16), (W_KC, BF16), (W_VC, BF16), (2 * W_KB, F32)]
    return pl.pallas_call(
        _proj_kernel,
        grid=(b // NB, nblk),
        in_specs=[
            tok(d),
            pl.BlockSpec((NB, None, ADA_CHUNKS, d), lambda bi, i: (bi, jnp.minimum(i, 1), 0, 0)),
            full(g_pre), full(w_proj), full(bd), full(qkg), full(wdec), full(bdec),
            rope, rope, rope,
        ],
        out_specs=[tok(w) for w, _ in widths],
        out_shape=[jax.ShapeDtypeStruct((b, s, w), dt) for w, dt in widths],
        compiler_params=_cparams(2),
        name="proj",
    )(h, mods, g_pre, w_proj, bd, qkg, wdec, bdec, rope_c, rope_sa, rope_sb)


KEY_TILE = 256
N_SBUF = 4
N_PBUF = 2


def _scores(qm, k_ref, grp, nk, s_ref, m_ref):
    pm = None
    for t in range(nk // KEY_TILE):
        ks = slice(t * KEY_TILE, (t + 1) * KEY_TILE)
        st = _dot_nt(qm, k_ref[ks, grp])
        s_ref[:, ks] = st
        for j in range(KEY_TILE // LANES):
            part = st[:, j * LANES:(j + 1) * LANES]
            pm = part if pm is None else jnp.maximum(pm, part)
    m_ref[...] = jnp.broadcast_to(jnp.max(pm, axis=-1, keepdims=True), (TOK, LANES))


def _exp_rows(s_ref, m_ref, e_ref, ps_ref, nk):
    for r in range(TOK // ROW_CHUNK):
        rows = slice(r * ROW_CHUNK, (r + 1) * ROW_CHUNK)
        mb = m_ref[rows, :]
        acc = None
        for j in range(nk // LANES):
            cols = slice(j * LANES, (j + 1) * LANES)
            e = jnp.exp2(s_ref[rows, cols] - mb)
            acc = e if acc is None else acc + e
            e_ref[rows, cols] = e.astype(e_ref.dtype)
        ps_ref[rows, :] = acc
    return jnp.sum(ps_ref[...], axis=-1, keepdims=True)


def _diff_rows(e1_ref, e2_ref, r1_ref, r2_ref, p_ref, nk):
    for r in range(TOK // ROW_CHUNK):
        rows = slice(r * ROW_CHUNK, (r + 1) * ROW_CHUNK)
        a = r1_ref[rows, :]
        b = r2_ref[rows, :]
        for j in range(nk // LANES):
            cols = slice(j * LANES, (j + 1) * LANES)
            p_ref[rows, cols] = (e1_ref[rows, cols] * a - e2_ref[rows, cols] * b).astype(BF16)


def _attn_body(nk, lambda_init, qa_ref, qc_ref, ka_ref, va_ref, kc_ref, vc_ref, lam_ref, dg_ref,
               ya_ref, yc_ref, *scratch):
    s_refs, p_refs = scratch[0:N_SBUF], scratch[N_SBUF:N_SBUF + N_PBUF]
    small = scratch[N_SBUF + N_PBUF:]
    m_refs, ps_refs, r_refs = small[0:N_SBUF], small[N_SBUF:2 * N_SBUF], small[2 * N_SBUF:3 * N_SBUF]
    lo = _lane_lo_mask((TOK, LANES))
    lo_b = lo.astype(BF16)
    hi_b = 1.0 - lo_b
    half = lambda q_grp, first: q_grp * (lo_b if first else hi_b)
    all_lanes = slice(0, LANES)

    unit = 0
    for j in range(A_Q_HEADS // 2):
        q_grp = qa_ref[:, j * LANES:(j + 1) * LANES]
        halves = []
        for first in (True, False):
            sb, pb = unit % N_SBUF, unit % N_PBUF
            _scores(half(q_grp, first), ka_ref, all_lanes, nk, s_refs[sb], m_refs[sb])
            l = _exp_rows(s_refs[sb], m_refs[sb], p_refs[pb], ps_refs[sb], nk)
            halves.append(_dot(p_refs[pb][:, 0:nk], va_ref[0:nk, :]) * (1.0 / l))
            unit += 1
        ya_ref[:, j * LANES:(j + 1) * LANES] = jnp.where(lo, halves[0], halves[1]).astype(BF16)

    lp = lam_ref[...]
    lam = (jnp.exp(jnp.sum(lp[0:1] * lp[1:2], axis=-1, keepdims=True))
           - jnp.exp(jnp.sum(lp[2:3] * lp[3:4], axis=-1, keepdims=True)) + lambda_init)
    for h in range(C_HEADS):
        grp = slice(h * LANES, (h + 1) * LANES)
        q_grp = qc_ref[:, grp]
        b1, b2, pb = unit % N_SBUF, (unit + 1) % N_SBUF, (unit // 2) % N_PBUF
        _scores(half(q_grp, True), kc_ref, grp, nk, s_refs[b1], m_refs[b1])
        l1 = _exp_rows(s_refs[b1], m_refs[b1], s_refs[b1], ps_refs[b1], nk)
        r_refs[b1][...] = jnp.broadcast_to(1.0 / l1, (TOK, LANES))
        _scores(half(q_grp, False), kc_ref, grp, nk, s_refs[b2], m_refs[b2])
        l2 = _exp_rows(s_refs[b2], m_refs[b2], s_refs[b2], ps_refs[b2], nk)
        r_refs[b2][...] = jnp.broadcast_to(lam / l2, (TOK, LANES))
        _diff_rows(s_refs[b1], s_refs[b2], r_refs[b1], r_refs[b2], p_refs[pb], nk)
        o = _dot(p_refs[pb][:, 0:nk], vc_ref[0:nk, grp])
        yc_ref[:, grp] = (_rms(o) * dg_ref[...] * (1.0 - lambda_init)).astype(BF16)
        unit += 2


def _attn_kernel(n_ctx, first_blk, lambda_init, *refs):
    n_all = refs[2].shape[0]
    if first_blk == 0:
        blk = pl.program_id(1)

        @pl.when(blk == 0)
        def _():
            _attn_body(n_ctx, lambda_init, *refs)

        @pl.when(blk > 0)
        def _():
            _attn_body(n_all, lambda_init, *refs)
    else:
        _attn_body(n_all, lambda_init, *refs)


def _attn_call(qa, qc, ka, va, kc, vc, lam_p, diff_g, *, n_ctx, first_blk, lambda_init):
    b, s, _ = qa.shape
    nblk = s // TOK - first_blk
    tok = pl.BlockSpec((None, TOK, 512), lambda bi, i: (bi, i + first_blk, 0))
    out_tok = pl.BlockSpec((None, TOK, 512), lambda bi, i: (bi, i, 0))
    kv = lambda w: pl.BlockSpec((None, s, w), lambda bi, i: (bi, 0, 0))
    full = lambda a: pl.BlockSpec(a.shape, lambda bi, i: (0,) * a.ndim)
    return pl.pallas_call(
        functools.partial(_attn_kernel, n_ctx, first_blk, lambda_init),
        grid=(b, nblk),
        in_specs=[tok, tok, kv(W_KA), kv(W_VA), kv(W_KC), kv(W_VC), full(lam_p), full(diff_g)],
        out_specs=[out_tok, out_tok],
        out_shape=[jax.ShapeDtypeStruct((b, nblk * TOK, 512), BF16)] * 2,
        scratch_shapes=(
            [pltpu.VMEM((TOK, s), F32)] * N_SBUF + [pltpu.VMEM((TOK, s), BF16)] * N_PBUF
            + [pltpu.VMEM((TOK, LANES), F32)] * (3 * N_SBUF)),
        compiler_params=_cparams(2),
        name="attn",
    )(qa, qc, ka, va, kc, vc, lam_p, diff_g)


def _gla_kernel(reverse, final, *refs):
    if final:
        (q_ref, k_ref, v_ref, g_ref, ob_ref, gate_ref, ng_ref, o_ref, state_ref) = refs
    else:
        (q_ref, k_ref, v_ref, g_ref, o_ref, state_ref) = refs
    n_chunks = TOK // GLA_CHUNK
    dkw = B_HEADS * B_DK
    dvw = B_HEADS * B_DV

    @pl.when(pl.program_id(1) == 0)
    def _():
        state_ref[...] = jnp.zeros_like(state_ref)

    row = lax.broadcasted_iota(jnp.int32, (TOK, TOK), 0)
    col = lax.broadcasted_iota(jnp.int32, (TOK, TOK), 1)
    same = (row // GLA_CHUNK) == (col // GLA_CHUNK)
    tri = same & ((col >= row) if reverse else (col <= row))
    tri_b = tri.astype(BF16)
    tri_t = (same & ((row >= col) if reverse else (row <= col))).astype(BF16)
    same_b = same.astype(BF16)

    q = q_ref[...] * (B_DK ** -0.5)
    k = k_ref[...]
    v = v_ref[...]
    g = g_ref[...]
    g_hi, g_lo = _split2(g)
    cum = _dot(tri_b, g_hi) + _dot(tri_b, g_lo)
    q_df = q * jnp.exp(cum)
    q_d = q_df.astype(BF16)
    k_d = (k * jnp.exp(-cum)).astype(BF16)
    gt_hi, gt_lo = _split2(g.T)
    cum_t = _dot(gt_hi, tri_t) + _dot(gt_lo, tri_t)
    last_t = _dot(gt_hi, same_b) + _dot(gt_lo, same_b)
    k_end_t = k.T * jnp.exp(last_t - cum_t)
    sel = ((lax.broadcasted_iota(jnp.int32, (TOK, n_chunks * LANES), 0) // GLA_CHUNK)
           == (lax.broadcasted_iota(jnp.int32, (TOK, n_chunks * LANES), 1) // LANES)).astype(BF16)
    decay = jnp.exp(_dot(gt_hi, sel) + _dot(gt_lo, sel))

    lo = _lane_lo_mask((TOK, LANES))
    o_parts = []
    for h in range(B_HEADS):
        grp = slice((h // 2) * LANES, (h // 2 + 1) * LANES)
        qm = jnp.where(lo if h % 2 == 0 else ~lo, q_df[:, grp], 0.0).astype(BF16)
        att = jnp.where(tri, _dot_nt(qm, k_d[:, grp]), 0.0).astype(BF16)
        o_parts.append(_dot(att, v[:, h * B_DV:(h + 1) * B_DV]))
    o_intra = jnp.concatenate(o_parts, axis=1)

    tok_chunk = lax.broadcasted_iota(jnp.int32, (dkw, TOK), 1) // GLA_CHUNK
    outs = [None] * n_chunks
    order = range(n_chunks - 1, -1, -1) if reverse else range(n_chunks)
    for c in order:
        rows = slice(c * GLA_CHUNK, (c + 1) * GLA_CHUNK)
        outs[c] = o_intra[rows, :] + _dot(q_d[rows, :], state_ref[...].astype(BF16))
        inc = _dot(jnp.where(tok_chunk == c, k_end_t, 0.0).astype(BF16), v)
        dec = decay[:, c * LANES:(c + 1) * LANES]
        for h in range(B_HEADS):
            hr = slice(h * B_DK, (h + 1) * B_DK)
            hc = slice(h * B_DV, (h + 1) * B_DV)
            state_ref[hr, hc] = dec[hr, :] * state_ref[hr, hc] + inc[hr, hc]
    o = jnp.concatenate(outs, axis=0)

    if final:
        o = o + ob_ref[...]
        gate = _silu(gate_ref[...])
        parts = []
        for h in range(B_HEADS):
            hc = slice(h * B_DV, (h + 1) * B_DV)
            parts.append(_rms(o[:, hc]) * ng_ref[...] * gate[:, hc])
        o_ref[...] = jnp.concatenate(parts, axis=1).astype(o_ref.dtype)
    else:
        o_ref[...] = o


def _gla_call(qb, kb, vb, la, extra, *, reverse):
    b, s, _ = qb.shape
    nblk = s // TOK
    final = extra is not None
    if reverse:
        blk = lambda i: jnp.where(i == 0, 0, nblk - i)
    else:
        blk = lambda i: i
    tok = lambda w, cb=0: pl.BlockSpec((None, TOK, w), lambda bi, i: (bi, blk(i), cb))
    in_specs = [tok(W_QB), tok(W_KB), tok(W_VB), tok(W_KB, 1 if reverse else 0)]
    args = [qb, kb, vb, la]
    if final:
        ob, gb, ng = extra
        in_specs += [tok(W_VB), tok(W_GB), pl.BlockSpec(ng.shape, lambda bi, i: (0, 0))]
        args += [ob, gb, ng]
    return pl.pallas_call(
        functools.partial(_gla_kernel, reverse, final),
        grid=(b, nblk),
        in_specs=in_specs,
        out_specs=tok(W_VB),
        out_shape=jax.ShapeDtypeStruct((b, s, W_VB), BF16 if final else F32),
        scratch_shapes=[pltpu.VMEM((B_HEADS * B_DK, B_HEADS * B_DV), F32)],
        compiler_params=_cparams(2),
        name="gla_bwd" if reverse else "gla_fwd",
    )(*args)


def _merge_kernel(h_ref, mods_ref, g_pre_ref, g_post_ref, ya_ref, yb_ref, yc_ref,
                  wg_ref, wbr_ref, wo_ref, o_ref):
    d = h_ref.shape[2]
    u = _stack([_modulated(h_ref[n], mods_ref[n], g_pre_ref[...], 0, 1) for n in range(NB)]).astype(BF16)
    m = None
    for br, y_ref in enumerate((ya_ref, yb_ref, yc_ref)):
        gate = jax.nn.sigmoid(_dot(u, wg_ref[:, br * d:(br + 1) * d]))
        t = gate * _dot(_stack([y_ref[n] for n in range(NB)]), wbr_ref[br])
        m = t if m is None else m + t
    y = _rms(_dot(m.astype(BF16), wo_ref[...])) * g_post_ref[...]
    for n in range(NB):
        o_ref[n] = h_ref[n] + mods_ref[n][2:3, :] * y[n * TOK:(n + 1) * TOK, :]


def _merge_call(h, mods, g_pre, g_post, ya, yb, yc, w_gates, w_br, w_o, *, first_blk):
    b, s, d = h.shape
    nblk = s // TOK - first_blk
    tok = lambda w: pl.BlockSpec((NB, TOK, w), lambda bi, i: (bi, i + first_blk, 0))
    tok0 = lambda w: pl.BlockSpec((NB, TOK, w), lambda bi, i: (bi, i, 0))
    full = lambda a: pl.BlockSpec(a.shape, lambda bi, i: (0,) * a.ndim)
    return pl.pallas_call(
        _merge_kernel,
        grid=(b // NB, nblk),
        in_specs=[
            tok(d),
            pl.BlockSpec((NB, None, ADA_CHUNKS, d),
                         lambda bi, i: (bi, jnp.minimum(i + first_blk, 1), 0, 0)),
            full(g_pre), full(g_post), tok0(512), tok(512), tok0(512),
            full(w_gates), full(w_br), full(w_o),
        ],
        out_specs=tok0(d),
        out_shape=jax.ShapeDtypeStruct((b, nblk * TOK, d), F32),
        compiler_params=_cparams(2),
        name="merge",
    )(h, mods, g_pre, g_post, ya, yb, yc, w_gates, w_br, w_o)


FFN_COLS = 256


def _ffn_kernel(first_blk, last_blk, h_ref, hp_ref, hn_ref, mods_ref, g_pre_ref, g_post_ref,
                wup_ref, cw_ref, cb_ref, wdn_ref, o_ref, acc_ref):
    blk = pl.program_id(1) + first_blk
    g_pre = g_pre_ref[...]
    f = wdn_ref.shape[0]
    prev_ok = blk >= 2
    next_ok = (blk >= 1) & (blk < last_blk)
    ext = TOK + 2 * SUBLANES
    v_parts, v_ext_parts = [], []
    for n in range(NB):
        mods = mods_ref[n]
        v = _modulated(h_ref[n], mods, g_pre, 3, 4)
        v_prev = jnp.where(prev_ok, _modulated(hp_ref[n], mods, g_pre, 3, 4), 0.0)
        v_next = jnp.where(next_ok, _modulated(hn_ref[n], mods, g_pre, 3, 4), 0.0)
        v_parts.append(v)
        v_ext_parts += [v_prev, v, v_next]
    vb = _stack(v_parts).astype(BF16)
    v_ext = _stack(v_ext_parts).astype(BF16)
    for c in range(f // FFN_COLS):
        cols = slice(c * FFN_COLS, (c + 1) * FFN_COLS)
        gate = _dot(v_ext, wup_ref[:, cols])
        val = _dot(vb, wup_ref[:, f + c * FFN_COLS:f + (c + 1) * FFN_COLS])
        conv_parts = []
        for n in range(NB):
            p0 = n * ext + SUBLANES
            conv_parts.append(gate[p0 - 1:p0 - 1 + TOK, :] * cw_ref[0:1, cols]
                              + gate[p0:p0 + TOK, :] * cw_ref[1:2, cols]
                              + gate[p0 + 1:p0 + 1 + TOK, :] * cw_ref[2:3, cols])
        conv = _stack(conv_parts) + cb_ref[:, cols]
        act = (_silu(conv) * val).astype(BF16)
        part = _dot(act, wdn_ref[cols, :])
        if c == 0:
            acc_ref[...] = part
        else:
            acc_ref[...] += part
    y = _rms(acc_ref[...]) * g_post_ref[...]
    for n in range(NB):
        o_ref[n] = h_ref[n] + mods_ref[n][5:6, :] * y[n * TOK:(n + 1) * TOK, :]


def _ffn_call(h, mods, g_pre, g_post, w_up, conv_w, conv_b, w_dn, *, first_blk):
    b, s, d = h.shape
    nblk = s // TOK
    sub = TOK // SUBLANES
    n_sub = s // SUBLANES
    tok = pl.BlockSpec((NB, TOK, d), lambda bi, i: (bi, i, 0))
    full = lambda a: pl.BlockSpec(a.shape, lambda bi, i: (0,) * a.ndim)
    once = lambda a: pl.BlockSpec(a.shape, lambda bi, i: (0,) * a.ndim, pipeline_mode=pl.Buffered(1))
    prev = pl.BlockSpec((NB, SUBLANES, d), lambda bi, i: (bi, jnp.maximum(i * sub - 1, 0), 0))
    nxt = pl.BlockSpec((NB, SUBLANES, d), lambda bi, i: (bi, jnp.minimum((i + 1) * sub, n_sub - 1), 0))
    return pl.pallas_call(
        functools.partial(_ffn_kernel, first_blk, nblk - 1 + first_blk),
        grid=(b // NB, nblk),
        in_specs=[
            tok, prev, nxt,
            pl.BlockSpec((NB, None, ADA_CHUNKS, d),
                         lambda bi, i: (bi, jnp.minimum(i + first_blk, 1), 0, 0)),
            full(g_pre), full(g_post), once(w_up), full(conv_w), full(conv_b), once(w_dn),
        ],
        out_specs=tok,
        out_shape=jax.ShapeDtypeStruct((b, s, d), F32),
        scratch_shapes=[pltpu.VMEM((NB * TOK, d), F32)],
        compiler_params=_cparams(2),
        name="ffn",
    )(h, h, h, mods, g_pre, g_post, w_up, conv_w, conv_b, w_dn)


def _rope_tables(n_ctx, n_lat):
    rows = n_lat // GRID_W
    row = jnp.repeat(jnp.arange(rows, dtype=F32), GRID_W)
    col = jnp.tile(jnp.arange(GRID_W, dtype=F32), rows)
    inv_freq = ROPE_BASE ** (-jnp.arange(N_FREQ, dtype=F32) / N_FREQ)
    ang = jnp.concatenate([row[:, None] * inv_freq, col[:, None] * inv_freq], axis=-1)
    cos, sin = jnp.cos(ang), jnp.sin(ang)
    zeros = jnp.zeros_like(sin[:, :N_FREQ])

    def head(row_part, col_part_a, col_part_b):
        return jnp.concatenate(row_part + col_part_a + col_part_b, axis=-1)

    cr, cc = cos[:, :N_FREQ], cos[:, N_FREQ:]
    sr, sc = sin[:, :N_FREQ], sin[:, N_FREQ:]
    c = jnp.concatenate([cr, cr, cc, cc], axis=-1)
    sa = jnp.concatenate([zeros, sr, zeros, sc], axis=-1)
    sb = jnp.concatenate([-sr, zeros, -sc, zeros], axis=-1)
    pad = lambda t, fill: jnp.concatenate(
        [jnp.full((n_ctx, HEAD_DIM), fill, F32), t], axis=0)
    two = lambda t: jnp.concatenate([t, t], axis=-1)
    return two(pad(c, 1.0)), two(pad(sa, 0.0)), two(pad(sb, 0.0))


def _pair_heads(n_heads):
    half = n_heads // 2
    order = []
    for j in range(half):
        order += [j, half + j]
    return np.concatenate([np.arange(h * HEAD_DIM, (h + 1) * HEAD_DIM) for h in order])


def kernel(x, c, ctx, c_ctx, ada_w, ada_b, norm_g, w_in, qk_norm_a, gla_w_decay, gla_b_decay,
           gla_norm, diff_lambda, diff_norm, w_branch, w_out, w_ffn_in, ffn_conv_w, ffn_conv_b,
           w_ffn_out):
    bsz, n_lat, d = x.shape
    n_ctx = ctx.shape[1]
    depth = ada_w.shape[0]
    assert n_ctx == TOK and n_lat % TOK == 0 and n_lat % GRID_W == 0 and bsz % NB == 0
    ffn_dim = w_ffn_out.shape[1]
    assert ffn_dim % FFN_COLS == 0

    perm = _pair_heads(A_Q_HEADS)
    sp = np.cumsum([512, 128, 128, 256, 256, 512, 512, 2 * GLA_RANK, 512, 512, 512])
    qa_w, ka_w, va_w, qb_w, kb_w, vb_w, gb_w, rb_w, qc_w, kc_w, vc_w, gt_w = jnp.split(w_in, sp, axis=-1)
    rb_w = jnp.pad(rb_w, ((0, 0), (0, 0), (0, W_RB - 2 * GLA_RANK)))
    w_proj = jnp.concatenate(
        [qa_w[..., perm], ka_w, va_w, qb_w, kb_w, vb_w, gb_w, qc_w, kc_w, vc_w, rb_w], axis=-1).astype(BF16)
    w_gates = gt_w.astype(BF16)
    w_br = jnp.concatenate([w_branch[:, 0:1, perm, :], w_branch[:, 1:]], axis=1).astype(BF16)
    w_o = w_out.astype(BF16)
    w_up = w_ffn_in.astype(BF16)
    w_dn = w_ffn_out.astype(BF16)
    dk_all = B_HEADS * B_DK
    wdec = jnp.zeros((depth, W_RB, 2 * dk_all), F32)
    wdec = wdec.at[:, 0:GLA_RANK, 0:dk_all].set(gla_w_decay[:, 0])
    wdec = wdec.at[:, GLA_RANK:2 * GLA_RANK, dk_all:].set(gla_w_decay[:, 1]).astype(BF16)
    bdec = gla_b_decay.reshape(depth, 1, 2 * dk_all)
    qkg = jnp.tile(qk_norm_a, (1, 1, A_Q_HEADS))
    head_id = np.arange(W_QA) // HEAD_DIM
    bd = jnp.asarray(head_id[:, None] == head_id[None, :], BF16)
    rope_c, rope_sa, rope_sb = _rope_tables(n_ctx, n_lat)

    rows = -(-(bsz + 1) // SUBLANES) * SUBLANES
    cvec = jnp.zeros((rows, d), F32).at[:bsz].set(c).at[bsz].set(c_ctx)
    ada = _ada_call(cvec, ada_w, ada_b)
    lat = ada[:, :bsz].reshape(depth, bsz, 1, ADA_CHUNKS, d)
    ctx_m = jnp.broadcast_to(ada[:, bsz].reshape(depth, 1, 1, ADA_CHUNKS, d), lat.shape)
    mods_all = jnp.concatenate([ctx_m, lat], axis=2)

    h = jnp.concatenate([ctx, x], axis=1)
    for layer in range(depth):
        last = layer == depth - 1
        first_blk = 1 if last else 0
        lambda_init = 0.8 - 0.6 * math.exp(-0.3 * layer)
        mods = mods_all[layer]
        g = norm_g[layer][:, None, :]
        qa, ka, va, qb, kb, vb, gb, qc, kc, vc, la = _proj_call(
            h, mods, g[0], w_proj[layer], bd, qkg[layer], wdec[layer], bdec[layer],
            rope_c, rope_sa, rope_sb)
        ya, yc = _attn_call(qa, qc, ka, va, kc, vc, diff_lambda[layer], diff_norm[layer][None, :],
                            n_ctx=n_ctx, first_blk=first_blk, lambda_init=lambda_init)
        ob = _gla_call(qb, kb, vb, la, None, reverse=True)
        yb = _gla_call(qb, kb, vb, la, (ob, gb, gla_norm[layer][None, :]), reverse=False)
        h_mid = _merge_call(h, mods, g[0], g[1], ya, yb, yc, w_gates[layer], w_br[layer], w_o[layer],
                            first_blk=first_blk)
        h = _ffn_call(h_mid, mods, g[2], g[3], w_up[layer], ffn_conv_w[layer],
                      ffn_conv_b[layer][None, :], w_dn[layer], first_blk=first_blk)
    return h
```
